```python
import math
import numpy as np
import jax, jax.numpy as jnp
from jax import lax

D_MODEL = 1024
BATCH = 8
SEQ = 8192
DEPTH = 1

HEAD_DIM = 64
NSA_HEADS = 8
NSA_KV_HEADS = 2
NSA_GROUP = NSA_HEADS // NSA_KV_HEADS
CMP_LEN = 32
CMP_STRIDE = 16
CMP_HID = 128
SLC_LEN = 64
SLC_TOPN = 16
WINDOW = 512
DSA_HEADS = 8
DSA_KV_RANK = 128
IDX_HEADS = 8
IDX_DIM = 64
DSA_TOPK_MAX = 256
N_EXPERTS = 32
MOE_TOPK = 4
D_EXPERT = 1024
SWIGLU_LIMIT = 7.0
SWIGLU_ALPHA = 1.702
REL_BUCKETS = 32
REL_MAX_DIST = 128
N_BIAS_HEADS = NSA_HEADS + DSA_HEADS
Q_BLOCK = 128
MOE_ROW_BLOCK = 128
NORM_EPS = 1e-6
NEG = -1e30

kernel_name = 'hybrid_nsa_dsa_moe_block'


def _in_widths():
    kv = NSA_KV_HEADS * HEAD_DIM
    return (NSA_HEADS * HEAD_DIM, kv, kv, kv, kv, kv, kv, NSA_HEADS * 3,
            DSA_HEADS * HEAD_DIM, DSA_KV_RANK, IDX_HEADS * IDX_DIM, IDX_DIM, IDX_HEADS,
            D_MODEL, D_MODEL)


def _split_offsets():
    offs, acc = [], 0
    for w in _in_widths()[:-1]:
        acc += w
        offs.append(acc)
    return offs


def rms_norm(x, g):
    xf = x.astype(jnp.float32)
    y = xf * lax.rsqrt(jnp.mean(xf * xf, axis=-1, keepdims=True) + NORM_EPS)
    return (y * g.astype(jnp.float32)).astype(x.dtype)


def masked_softmax(logits, mask):
    p = jax.nn.softmax(jnp.where(mask, logits, NEG), axis=-1)
    return jnp.where(mask, p, 0.0)


def rel_bucket(rel):
    n = jnp.maximum(rel, 0)
    exact = REL_BUCKETS // 2
    nf = jnp.maximum(n, 1).astype(jnp.float32)
    large = exact + (jnp.log(nf / exact) / math.log(REL_MAX_DIST / exact) * (REL_BUCKETS - exact)).astype(jnp.int32)
    large = jnp.minimum(large, REL_BUCKETS - 1)
    return jnp.where(n < exact, n, large)


def compress_blocks(x, pe, w1, b1, w2):
    B, S, G, dk = x.shape
    n_c = (S - CMP_LEN) // CMP_STRIDE + 1
    idx = np.arange(n_c, dtype=np.int32)[:, None] * CMP_STRIDE + np.arange(CMP_LEN, dtype=np.int32)[None, :]
    blk = x[:, idx] + pe[:, None, :]
    flat = blk.transpose(0, 1, 3, 2, 4).reshape(B, n_c, G, CMP_LEN * dk)
    return jax.nn.gelu(flat @ w1 + b1) @ w2


def nsa_attention(q, kc, vc, ks, vs, kw, vw, gates, tab):
    B, S, G, hg, dk = q.shape
    n_c = kc.shape[1]
    n_s = S // SLC_LEN
    n_top = min(SLC_TOPN, n_s)
    scale = dk ** -0.5
    f32 = jnp.float32
    c_start = np.arange(n_c, dtype=np.int32) * CMP_STRIDE
    c_last = c_start + (CMP_LEN - 1)
    s_start = np.arange(n_s, dtype=np.int32) * SLC_LEN
    overlap = jnp.asarray(((c_start[:, None] < s_start[None, :] + SLC_LEN) &
                           (c_start[:, None] + CMP_LEN > s_start[None, :])).astype(np.float32))
    ks_blk = ks.reshape(B, n_s, SLC_LEN, G, dk).transpose(0, 3, 1, 2, 4)
    vs_blk = vs.reshape(B, n_s, SLC_LEN, G, dk).transpose(0, 3, 1, 2, 4)
    kw_pad = jnp.pad(kw, ((0, 0), (WINDOW, 0), (0, 0), (0, 0)))
    vw_pad = jnp.pad(vw, ((0, 0), (WINDOW, 0), (0, 0), (0, 0)))
    n_win = Q_BLOCK + WINDOW
    rel_w = jnp.arange(Q_BLOCK)[:, None] - (jnp.arange(n_win) - WINDOW)[None, :]
    in_band = (rel_w >= 0) & (rel_w < WINDOW)
    bias_w = tab[rel_bucket(rel_w)].reshape(Q_BLOCK, n_win, G, hg).transpose(0, 2, 3, 1)
    tab_g = tab.reshape(REL_BUCKETS, G, hg).transpose(1, 0, 2)
    g_idx = jnp.arange(G)[None, :, None, None]
    blk_ids = jnp.arange(n_s, dtype=jnp.int32)
    take = jax.vmap(jax.vmap(lambda blocks, ix: blocks[ix]))

    def one_block(i):
        t0 = i * Q_BLOCK
        tpos = t0 + jnp.arange(Q_BLOCK, dtype=jnp.int32)
        qb = lax.dynamic_slice_in_dim(q, t0, Q_BLOCK, axis=1)
        gb = lax.dynamic_slice_in_dim(gates, t0, Q_BLOCK, axis=1)
        rel_c = tpos[:, None] - c_last[None, :]
        bias_c = tab[rel_bucket(rel_c)].reshape(Q_BLOCK, n_c, G, hg).transpose(0, 2, 3, 1)
        s_c = jnp.einsum('btghd,bcgd->btghc', qb, kc).astype(f32) * scale + bias_c
        p_c = masked_softmax(s_c, (rel_c >= 0)[:, None, None, :])
        o_c = jnp.einsum('btghc,bcgd->btghd', p_c.astype(vc.dtype), vc)
        imp = jnp.einsum('btghc,cj->btgj', p_c, overlap)
        cur = tpos // SLC_LEN
        forced = (blk_ids[None, :] == 0) | (blk_ids[None, :] == cur[:, None]) | (blk_ids[None, :] == cur[:, None] - 1)
        visible = blk_ids[None, :] <= cur[:, None]
        imp = jnp.where(forced[None, :, None, :], 1e6, jnp.where(visible[None, :, None, :], imp, -1e6))
        sel = lax.top_k(imp, n_top)[1].transpose(0, 2, 1, 3)
        n_k = n_top * SLC_LEN
        k_sel = take(ks_blk, sel).reshape(B, G, Q_BLOCK, n_k, dk)
        v_sel = take(vs_blk, sel).reshape(B, G, Q_BLOCK, n_k, dk)
        pos_sel = (sel[..., None] * SLC_LEN + jnp.arange(SLC_LEN, dtype=jnp.int32)).reshape(B, G, Q_BLOCK, n_k)
        rel_s = tpos[None, None, :, None] - pos_sel
        bias_s = tab_g[g_idx, rel_bucket(rel_s)].transpose(0, 1, 2, 4, 3)
        qg = qb.transpose(0, 2, 1, 3, 4)
        s_s = jnp.einsum('bgthd,bgtnd->bgthn', qg, k_sel).astype(f32) * scale + bias_s
        p_s = masked_softmax(s_s, (rel_s >= 0)[:, :, :, None, :])
        o_s = jnp.einsum('bgthn,bgtnd->bgthd', p_s.astype(v_sel.dtype), v_sel).transpose(0, 2, 1, 3, 4)
        kwb = lax.dynamic_slice_in_dim(kw_pad, t0, n_win, axis=1)
        vwb = lax.dynamic_slice_in_dim(vw_pad, t0, n_win, axis=1)
        kpos = t0 - WINDOW + jnp.arange(n_win, dtype=jnp.int32)
        mask_w = in_band & (kpos >= 0)[None, :]
        s_w = jnp.einsum('btghd,bkgd->btghk', qb, kwb).astype(f32) * scale + bias_w
        p_w = masked_softmax(s_w, mask_w[:, None, None, :])
        o_w = jnp.einsum('btghk,bkgd->btghd', p_w.astype(vwb.dtype), vwb)
        o = gb[..., 0:1] * o_c + gb[..., 1:2] * o_s + gb[..., 2:3] * o_w
        return o.reshape(B, Q_BLOCK, G * hg * dk)

    out = lax.map(one_block, jnp.arange(S // Q_BLOCK, dtype=jnp.int32))
    return out.transpose(1, 0, 2, 3).reshape(B, S, G * hg * dk)


def dsa_attention(q, k, v, iq, ik, iw, tab):
    B, S, H, dk = q.shape
    n_keep = min(DSA_TOPK_MAX, S // 4)
    scale = dk ** -0.5
    idx_scale = IDX_DIM ** -0.5
    f32 = jnp.float32
    key_pos = jnp.arange(S, dtype=jnp.int32)
    take = jax.vmap(lambda rows, ix: rows[ix])

    def one_block(i):
        t0 = i * Q_BLOCK
        tpos = t0 + jnp.arange(Q_BLOCK, dtype=jnp.int32)
        qb = lax.dynamic_slice_in_dim(q, t0, Q_BLOCK, axis=1)
        iqb = lax.dynamic_slice_in_dim(iq, t0, Q_BLOCK, axis=1)
        iwb = lax.dynamic_slice_in_dim(iw, t0, Q_BLOCK, axis=1)
        act = jax.nn.relu(jnp.einsum('bthd,bsd->bths', iqb, ik).astype(f32) * idx_scale)
        score = jnp.einsum('bths,bth->bts', act, iwb.astype(f32))
        causal = tpos[:, None] >= key_pos[None, :]
        score = jnp.where(causal[None], score, NEG)
        sel = lax.top_k(score, n_keep)[1]
        k_sel = take(k, sel)
        v_sel = take(v, sel)
        rel = tpos[None, :, None] - sel
        bias = tab[rel_bucket(rel)].transpose(0, 1, 3, 2)
        s = jnp.einsum('bthd,btnd->bthn', qb, k_sel).astype(f32) * scale + bias
        p = masked_softmax(s, (rel >= 0)[:, :, None, :])
        o = jnp.einsum('bthn,btnd->bthd', p.astype(v_sel.dtype), v_sel)
        return o.reshape(B, Q_BLOCK, H * dk)

    out = lax.map(one_block, jnp.arange(S // Q_BLOCK, dtype=jnp.int32))
    return out.transpose(1, 0, 2, 3).reshape(B, S, H * dk)


def clamped_swiglu(hu):
    glu, lin = hu[..., ::2], hu[..., 1::2]
    glu = jnp.minimum(glu, SWIGLU_LIMIT)
    lin = jnp.clip(lin, -SWIGLU_LIMIT, SWIGLU_LIMIT)
    return glu * jax.nn.sigmoid(SWIGLU_ALPHA * glu) * (lin + 1.0)


def moe_ffn(h, w_router, b_router, w_gate_up, b_gate_up, w_down, b_down):
    B, S, D = h.shape
    N = B * S
    NK = N * MOE_TOPK
    R = MOE_ROW_BLOCK
    xf = h.reshape(N, D)
    logits = (xf @ w_router).astype(jnp.float32) + b_router.astype(jnp.float32)
    top_val, top_e = lax.top_k(logits, MOE_TOPK)
    gate = jax.nn.softmax(top_val, axis=-1)
    flat_e = top_e.reshape(-1)
    flat_tok = jnp.repeat(jnp.arange(N, dtype=jnp.int32), MOE_TOPK)
    order = jnp.argsort(flat_e)
    se, st, sg = flat_e[order], flat_tok[order], gate.reshape(-1)[order]
    counts = jnp.bincount(flat_e, length=N_EXPERTS)
    padded = (counts + R - 1) // R * R
    pad_end = jnp.cumsum(padded)
    pad_start = pad_end - padded
    start = jnp.cumsum(counts) - counts
    dest = pad_start[se] + (jnp.arange(NK, dtype=jnp.int32) - start[se])
    n_rows = (NK + R - 1) // R * R + N_EXPERTS * R
    n_blocks = n_rows // R
    row_tok = jnp.full((n_rows,), N, jnp.int32).at[dest].set(st)
    row_gate = jnp.zeros((n_rows,), jnp.float32).at[dest].set(sg)
    blk_e = jnp.minimum(jnp.searchsorted(pad_end, jnp.arange(n_blocks, dtype=jnp.int32) * R, side='right'), N_EXPERTS - 1)
    x_pad = jnp.concatenate([xf, jnp.zeros((1, D), xf.dtype)], axis=0)

    def run_block(args):
        tok, g, e = args
        xb = x_pad[tok]
        hu = xb @ w_gate_up[e] + b_gate_up[e]
        y = clamped_swiglu(hu) @ w_down[e] + b_down[e]
        return y * g[:, None].astype(y.dtype)

    ys = lax.map(run_block, (row_tok.reshape(n_blocks, R), row_gate.reshape(n_blocks, R), blk_e))
    out = jnp.zeros((N + 1, D), ys.dtype).at[row_tok].add(ys.reshape(n_rows, D))
    return out[:N].reshape(B, S, D)


def hybrid_layer(x, attn_norm, w_in, nsa_q_norm, nsa_k_norm, cmp_k_pe, cmp_k_w1, cmp_k_b1, cmp_k_w2,
                 cmp_v_pe, cmp_v_w1, cmp_v_b1, cmp_v_w2, dsa_kv_norm, dsa_w_uk, dsa_w_uv, dsa_q_norm,
                 dsa_k_norm, w_branch_a, w_branch_b, w_out, rel_bias, ffn_norm, w_router, b_router,
                 w_gate_up, b_gate_up, w_down, b_down):
    B, S, _ = x.shape
    G, hg, dk = NSA_KV_HEADS, NSA_GROUP, HEAD_DIM
    h = rms_norm(x, attn_norm)
    (a_q, c_k, c_v, s_k, s_v, w_k, w_v, a_g, b_q, b_ckv, i_q, i_k, i_w, g_a, g_b) = jnp.split(h @ w_in, _split_offsets(), axis=-1)
    kv_shape = (B, S, G, dk)
    q_a = rms_norm(a_q.reshape(B, S, NSA_HEADS, dk), nsa_q_norm).reshape(B, S, G, hg, dk)
    kc = rms_norm(compress_blocks(c_k.reshape(kv_shape), cmp_k_pe, cmp_k_w1, cmp_k_b1, cmp_k_w2), nsa_k_norm)
    vc = compress_blocks(c_v.reshape(kv_shape), cmp_v_pe, cmp_v_w1, cmp_v_b1, cmp_v_w2)
    ks = rms_norm(s_k.reshape(kv_shape), nsa_k_norm)
    kw = rms_norm(w_k.reshape(kv_shape), nsa_k_norm)
    branch_gates = jax.nn.sigmoid(a_g).reshape(B, S, G, hg, 3)
    o_a = nsa_attention(q_a, kc, vc, ks, s_v.reshape(kv_shape), kw, w_v.reshape(kv_shape), branch_gates, rel_bias[:, :NSA_HEADS])
    q_b = rms_norm(b_q.reshape(B, S, DSA_HEADS, dk), dsa_q_norm)
    ckv = rms_norm(b_ckv, dsa_kv_norm)
    k_b = rms_norm(ckv @ dsa_w_uk, dsa_k_norm)
    v_b = ckv @ dsa_w_uv
    o_b = dsa_attention(q_b, k_b, v_b, i_q.reshape(B, S, IDX_HEADS, IDX_DIM), i_k, i_w * IDX_HEADS ** -0.5, rel_bias[:, NSA_HEADS:])
    merged = jax.nn.sigmoid(g_a) * (o_a @ w_branch_a) + jax.nn.sigmoid(g_b) * (o_b @ w_branch_b)
    x = x + merged @ w_out
    return x + moe_ffn(rms_norm(x, ffn_norm), w_router, b_router, w_gate_up, b_gate_up, w_down, b_down)


def setup_inputs(seed: int = 0) -> dict:
    key = jax.random.key(seed)
    ks = jax.random.split(key, 32)
    L, D, dk = DEPTH, D_MODEL, HEAD_DIM
    d_in = sum(_in_widths())
    f32 = jnp.float32

    def nrm(k, shape, scale):
        return jax.random.normal(k, shape, f32) * scale

    def gain(k, n):
        return 1.0 + 0.02 * jax.random.normal(k, (L, n), f32)

    return {
        'x': jax.random.normal(ks[0], (BATCH, SEQ, D), f32),
        'attn_norm': gain(ks[1], D),
        'w_in': nrm(ks[2], (L, D, d_in), D ** -0.5),
        'nsa_q_norm': gain(ks[3], dk),
        'nsa_k_norm': gain(ks[4], dk),
        'cmp_k_pe': nrm(ks[5], (L, CMP_LEN, dk), 0.1),
        'cmp_k_w1': nrm(ks[6], (L, CMP_LEN * dk, CMP_HID), (CMP_LEN * dk) ** -0.5),
        'cmp_k_b1': nrm(ks[7], (L, CMP_HID), 0.01),
        'cmp_k_w2': nrm(ks[8], (L, CMP_HID, dk), CMP_HID ** -0.5),
        'cmp_v_pe': nrm(ks[9], (L, CMP_LEN, dk), 0.1),
        'cmp_v_w1': nrm(ks[10], (L, CMP_LEN * dk, CMP_HID), (CMP_LEN * dk) ** -0.5),
        'cmp_v_b1': nrm(ks[11], (L, CMP_HID), 0.01),
        'cmp_v_w2': nrm(ks[12], (L, CMP_HID, dk), CMP_HID ** -0.5),
        'dsa_kv_norm': gain(ks[13], DSA_KV_RANK),
        'dsa_w_uk': nrm(ks[14], (L, DSA_KV_RANK, dk), DSA_KV_RANK ** -0.5),
        'dsa_w_uv': nrm(ks[15], (L, DSA_KV_RANK, dk), DSA_KV_RANK ** -0.5),
        'dsa_q_norm': gain(ks[16], dk),
        'dsa_k_norm': gain(ks[17], dk),
        'w_branch_a': nrm(ks[18], (L, NSA_HEADS * dk, D), (NSA_HEADS * dk) ** -0.5),
        'w_branch_b': nrm(ks[19], (L, DSA_HEADS * dk, D), (DSA_HEADS * dk) ** -0.5),
        'w_out': nrm(ks[20], (L, D, D), D ** -0.5),
        'rel_bias': nrm(ks[21], (REL_BUCKETS, N_BIAS_HEADS), 0.2),
        'ffn_norm': gain(ks[22], D),
        'w_router': nrm(ks[23], (L, D, N_EXPERTS), D ** -0.5),
        'b_router': nrm(ks[24], (L, N_EXPERTS), 0.01),
        'w_gate_up': nrm(ks[25], (L, N_EXPERTS, D, 2 * D_EXPERT), D ** -0.5),
        'b_gate_up': nrm(ks[26], (L, N_EXPERTS, 2 * D_EXPERT), 0.01),
        'w_down': nrm(ks[27], (L, N_EXPERTS, D_EXPERT, D), D_EXPERT ** -0.5),
        'b_down': nrm(ks[28], (L, N_EXPERTS, D), 0.01),
    }


def reference(x, attn_norm, w_in, nsa_q_norm, nsa_k_norm, cmp_k_pe, cmp_k_w1, cmp_k_b1, cmp_k_w2,
              cmp_v_pe, cmp_v_w1, cmp_v_b1, cmp_v_w2, dsa_kv_norm, dsa_w_uk, dsa_w_uv, dsa_q_norm,
              dsa_k_norm, w_branch_a, w_branch_b, w_out, rel_bias, ffn_norm, w_router, b_router,
              w_gate_up, b_gate_up, w_down, b_down):
    for l in range(DEPTH):
        x = hybrid_layer(x, attn_norm[l], w_in[l], nsa_q_norm[l], nsa_k_norm[l], cmp_k_pe[l], cmp_k_w1[l],
                         cmp_k_b1[l], cmp_k_w2[l], cmp_v_pe[l], cmp_v_w1[l], cmp_v_b1[l], cmp_v_w2[l],
                         dsa_kv_norm[l], dsa_w_uk[l], dsa_w_uv[l], dsa_q_norm[l], dsa_k_norm[l],
                         w_branch_a[l], w_branch_b[l], w_out[l], rel_bias, ffn_norm[l], w_router[l],
                         b_router[l], w_gate_up[l], b_gate_up[l], w_down[l], b_down[l])
    return x
```

```python
import functools
import math

import numpy as np
import jax
import jax.numpy as jnp
from jax import lax
from jax.experimental import pallas as pl
from jax.experimental.pallas import tpu as pltpu

D_MODEL = 1024
HEAD_DIM = 64
NSA_HEADS = 8
NSA_KV_HEADS = 2
NSA_GROUP = NSA_HEADS // NSA_KV_HEADS
CMP_LEN = 32
CMP_STRIDE = 16
CMP_HID = 128
SLC_LEN = 64
SLC_TOPN = 16
WINDOW = 512
DSA_HEADS = 8
DSA_KV_RANK = 128
IDX_HEADS = 8
IDX_DIM = 64
DSA_TOPK_MAX = 256
N_EXPERTS = 32
MOE_TOPK = 4
D_EXPERT = 1024
SWIGLU_LIMIT = 7.0
SWIGLU_ALPHA = 1.702
REL_BUCKETS = 32
REL_MAX_DIST = 128
NORM_EPS = 1e-6
NEG = -1e30

Q_BLOCK = 128
KEY_CHUNK = 256
LANES = 128
VMEM_LIMIT = 56 * 1024 * 1024

F32 = jnp.float32
BF16 = jnp.bfloat16
I32 = jnp.int32


def _params(*sem):
    return pltpu.CompilerParams(dimension_semantics=sem, vmem_limit_bytes=VMEM_LIMIT)


def _dot(a, b):
    return jnp.dot(a, b, preferred_element_type=F32)


def _dot_nt(a, b):
    return lax.dot_general(a, b, (((1,), (1,)), ((), ())), preferred_element_type=F32)


def _const_spec(shape):
    nd = len(shape)
    return pl.BlockSpec(shape, lambda *_: (0,) * nd)


def _segment_sumsq(y, seg_ones):
    y2 = y * y
    hi = y2.astype(BF16)
    lo = (y2 - hi.astype(F32)).astype(BF16)
    return _dot(hi, seg_ones) + _dot(lo, seg_ones)


def _head_rms(y, seg_ones, gain):
    ms = _segment_sumsq(y, seg_ones) * (1.0 / HEAD_DIM)
    return y * lax.rsqrt(ms + NORM_EPS) * gain


def _row_rms(y, gain):
    ms = jnp.mean(y * y, axis=-1, keepdims=True)
    return y * lax.rsqrt(ms + NORM_EPS) * gain


def _in_proj_kernel(x_ref, g_ref, wqa_ref, wkv_ref, wmisc_ref, wqb_ref, wckv_ref, wiq_ref, wik_ref,
                    wga_ref, wgb_ref, seg512_ref, seg128_ref, gqa_ref, gka_ref, gqb_ref, gckv_ref,
                    wuk_ref, wuv_ref, gkb_ref,
                    qa_ref, ck_ref, cv_ref, ks_ref, sv_ref, kw_ref, wv_ref, misc_ref, qb_ref,
                    kb_ref, vb_ref, iq_ref, ik_ref, ga_ref, gb_ref):
    h = _row_rms(x_ref[...], g_ref[...]).astype(BF16)
    scale = HEAD_DIM ** -0.5
    seg512 = seg512_ref[...]
    seg128 = seg128_ref[...]

    qa = _head_rms(_dot(h, wqa_ref[...]), seg512, gqa_ref[...])
    qa_ref[...] = (qa * scale).astype(BF16)

    kv = _dot(h, wkv_ref[...])
    ck_ref[...] = kv[:, 0:128]
    cv_ref[...] = kv[:, 128:256]
    ks_ref[...] = _head_rms(kv[:, 256:384], seg128, gka_ref[...]).astype(BF16)
    sv_ref[...] = kv[:, 384:512].astype(BF16)
    kw_ref[...] = _head_rms(kv[:, 512:640], seg128, gka_ref[...]).astype(BF16)
    wv_ref[...] = kv[:, 640:768].astype(BF16)

    misc = _dot(h, wmisc_ref[...])
    lane = lax.broadcasted_iota(I32, misc.shape, 1)
    iw_scale = (IDX_HEADS ** -0.5)
    idx_scale = IDX_DIM ** -0.5
    misc_ref[...] = jnp.where(lane < NSA_HEADS * 3, jax.nn.sigmoid(misc), (misc * iw_scale) * idx_scale)

    qb = _head_rms(_dot(h, wqb_ref[...]), seg512, gqb_ref[...])
    qb_ref[...] = (qb * scale).astype(BF16)

    ckv = _row_rms(_dot(h, wckv_ref[...]), gckv_ref[...]).astype(BF16)
    kb_ref[...] = _row_rms(_dot(ckv, wuk_ref[...]), gkb_ref[...]).astype(BF16)
    vb_ref[...] = _dot(ckv, wuv_ref[...]).astype(BF16)

    iq_ref[...] = _dot(h, wiq_ref[...]).astype(BF16)
    ik_ref[...] = _dot(h, wik_ref[...]).astype(BF16)
    ga_ref[...] = jax.nn.sigmoid(_dot(h, wga_ref[...]))
    gb_ref[...] = jax.nn.sigmoid(_dot(h, wgb_ref[...]))


def _in_proj(xf, attn_norm, w_in, nsa_q_norm, nsa_k_norm, dsa_q_norm, dsa_kv_norm, dsa_w_uk, dsa_w_uv,
             dsa_k_norm, tm):
    n = xf.shape[0]
    kvw = NSA_KV_HEADS * HEAD_DIM
    widths = (NSA_HEADS * HEAD_DIM, kvw, kvw, kvw, kvw, kvw, kvw, NSA_HEADS * 3,
              DSA_HEADS * HEAD_DIM, DSA_KV_RANK, IDX_HEADS * IDX_DIM, IDX_DIM, IDX_HEADS, D_MODEL, D_MODEL)
    offs = np.concatenate([[0], np.cumsum(widths)])
    sec = [w_in[:, offs[i]:offs[i + 1]] for i in range(len(widths))]
    wb = lambda w: w.astype(BF16)
    wqa = wb(sec[0])
    wkv = wb(jnp.concatenate(sec[1:7], axis=1))
    wmisc = wb(jnp.concatenate([sec[7], sec[12], jnp.zeros((D_MODEL, LANES - 32), F32)], axis=1))
    wqb, wckv, wiq, wik, wga, wgb = wb(sec[8]), wb(sec[9]), wb(sec[10]), wb(sec[11]), wb(sec[13]), wb(sec[14])

    def seg_ones(w):
        i = np.arange(w) // HEAD_DIM
        return jnp.asarray((i[:, None] == i[None, :]).astype(np.float32), BF16)

    consts = [attn_norm.reshape(1, -1), wqa, wkv, wmisc, wqb, wckv, wiq, wik, wga, wgb,
              seg_ones(512), seg_ones(128),
              jnp.tile(nsa_q_norm, NSA_HEADS).reshape(1, -1), jnp.tile(nsa_k_norm, NSA_KV_HEADS).reshape(1, -1),
              jnp.tile(dsa_q_norm, DSA_HEADS).reshape(1, -1), dsa_kv_norm.reshape(1, -1),
              wb(dsa_w_uk), wb(dsa_w_uv), dsa_k_norm.reshape(1, -1)]
    out_defs = [(512, BF16), (128, F32), (128, F32), (128, BF16), (128, BF16), (128, BF16), (128, BF16),
                (128, F32), (512, BF16), (64, BF16), (64, BF16), (512, BF16), (64, BF16),
                (D_MODEL, F32), (D_MODEL, F32)]
    row = lambda w: pl.BlockSpec((tm, w), lambda i: (i, 0))
    return pl.pallas_call(
        _in_proj_kernel,
        grid=(n // tm,),
        in_specs=[row(D_MODEL)] + [_const_spec(c.shape) for c in consts],
        out_specs=[row(w) for w, _ in out_defs],
        out_shape=[jax.ShapeDtypeStruct((n, w), dt) for w, dt in out_defs],
        compiler_params=_params("parallel"),
        name="in_proj",
    )(xf, *consts)


def _compress_kernel(c_ref, pea_ref, peb_ref, w1a_ref, w1b_ref, b1_ref, w2_ref, gain_ref, o_ref, *, norm, n_c):
    c = c_ref[0, 0]
    ua = _dot((c + pea_ref[...]).astype(BF16), w1a_ref[...])
    ub = _dot((c + peb_ref[...]).astype(BF16), w1b_ref[...])
    rows = c.shape[0]
    ub_next = pltpu.roll(ub, rows - 1, 0)
    hid = jax.nn.gelu(ua + ub_next + b1_ref[...])
    o = _dot(hid.astype(BF16), w2_ref[...])
    if norm:
        o = _row_rms(o, gain_ref[...])
    ridx = lax.broadcasted_iota(I32, o.shape, 0)
    o_ref[0, 0] = jnp.where(ridx < n_c, o, 0.0).astype(o_ref.dtype)


def _compress(c, pe, w1, b1, w2, gain, norm):
    b, g, rows, width = c.shape
    half = CMP_STRIDE * HEAD_DIM
    consts = [pe[:CMP_STRIDE].reshape(1, half), pe[CMP_STRIDE:].reshape(1, half),
              w1[:half].astype(BF16), w1[half:].astype(BF16), b1.reshape(1, -1), w2.astype(BF16),
              gain.reshape(1, -1)]
    return pl.pallas_call(
        functools.partial(_compress_kernel, norm=norm, n_c=rows - 1),
        grid=(b, g),
        in_specs=[pl.BlockSpec((1, 1, rows, width), lambda i, j: (i, j, 0, 0))] + [_const_spec(x.shape) for x in consts],
        out_specs=pl.BlockSpec((1, 1, rows, HEAD_DIM), lambda i, j: (i, j, 0, 0)),
        out_shape=jax.ShapeDtypeStruct((b, g, rows, HEAD_DIM), BF16),
        compiler_params=_params("parallel", "parallel"),
        name="compress_norm" if norm else "compress",
    )(c, *consts)


def _online_update(carry, s, v):
    m, l, acc = carry
    m_new = jnp.maximum(m, jnp.max(s, axis=1, keepdims=True))
    alpha = jnp.exp(m - m_new)
    p = jnp.exp(s - m_new)
    l = alpha * l + jnp.sum(p, axis=1, keepdims=True)
    acc = alpha * acc + _dot(p.astype(BF16), v)
    return m_new, l, acc


def _online_init(rows):
    return (jnp.full((rows, 1), NEG, F32), jnp.zeros((rows, 1), F32), jnp.zeros((rows, HEAD_DIM), F32))


def _online_finish(carry):
    m, l, acc = carry
    return acc / l


def _nsa_kernel(q_ref, kc_ref, vc_ref, ks_ref, vs_ref, kw_ref, vw_ref, gate_ref, biasc_ref, tdiag_ref,
                ovl_ref, o_ref):
    qi = pl.program_id(2)
    hg, T, C = NSA_GROUP, Q_BLOCK, KEY_CHUNK
    R = hg * T
    t0 = qi * T
    q = q_ref[0].reshape(R, HEAD_DIM)
    tdiag = tdiag_ref[0]
    col = lax.broadcasted_iota(I32, (T, C), 1)
    tpos_c = t0 + lax.broadcasted_iota(I32, (T, C), 0)
    diag_start = pl.multiple_of(jnp.maximum(t0 - T, 0), T)

    s_c = _dot_nt(q, kc_ref[0, 0]) + biasc_ref[0].reshape(R, -1)
    m_c = jnp.max(s_c, axis=1, keepdims=True)
    p_c = jnp.where(s_c > 0.5 * NEG, jnp.exp(s_c - m_c), 0.0)
    l_c = jnp.sum(p_c, axis=1, keepdims=True)
    p_c = p_c / jnp.where(l_c > 0.0, l_c, 1.0)
    p_cb = p_c.astype(BF16)
    o_c = _dot(p_cb, vc_ref[0, 0])

    imp_h = _dot(p_cb, ovl_ref[...])
    imp = imp_h[0:T]
    for h in range(1, hg):
        imp = imp + imp_h[h * T:(h + 1) * T]
    n_s = imp.shape[1]
    blk = lax.broadcasted_iota(I32, (T, n_s), 1)
    blk_f = blk.astype(F32)
    cur = (t0 + lax.broadcasted_iota(I32, (T, n_s), 0)) // SLC_LEN
    forced = (blk == 0) | (blk == cur) | (blk == cur - 1)
    work = jnp.where(forced, 1e6, jnp.where(blk <= cur, imp, -1e6))

    def pick(_, carry):
        work, sel = carry
        mx = jnp.max(work, axis=1, keepdims=True)
        first = jnp.min(jnp.where(work == mx, blk_f, float(n_s)), axis=1, keepdims=True)
        hit = blk_f == first
        return jnp.where(hit, -3e38, work), jnp.where(hit, 1.0, sel)

    _, sel = lax.fori_loop(0, min(SLC_TOPN, n_s), pick, (work, jnp.zeros((T, n_s), F32)))
    sel_b = sel.astype(BF16)
    e_row = lax.broadcasted_iota(I32, (n_s, C), 0)
    e_col = lax.broadcasted_iota(I32, (n_s, C), 1) // SLC_LEN

    def sel_step(start, carry, table):
        k = ks_ref[0, 0, pl.ds(start, C), :]
        v = vs_ref[0, 0, pl.ds(start, C), :]
        expand = jnp.where(e_row == e_col + start // SLC_LEN, 1.0, 0.0).astype(BF16)
        keep = _dot(sel_b, expand) > 0.5
        s = _dot_nt(q, k).reshape(hg, T, C)
        if table is None:
            keep = keep & (col + start < t0 - T)
        else:
            s = s + table
        s = jnp.where(keep[None], s, NEG)
        return _online_update(carry, s.reshape(R, C), v)

    carry = lax.fori_loop(0, qi // 2, lambda j, c: sel_step(pl.multiple_of(j * C, C), c, None),
                          _online_init(R))
    o_s = _online_finish(sel_step(diag_start, carry, tdiag))

    def win_step(lo, carry, table):
        start = pl.multiple_of(jnp.maximum(lo, 0), T)
        k = kw_ref[0, 0, pl.ds(start, C), :]
        v = vw_ref[0, 0, pl.ds(start, C), :]
        pos = col + start
        rel = tpos_c - pos
        keep = (pos < lo + C) & (rel >= 0) & (rel < WINDOW)
        s = _dot_nt(q, k).reshape(hg, T, C)
        if table is not None:
            s = s + table
        s = jnp.where(keep[None], s, NEG)
        return _online_update(carry, s.reshape(R, C), v)

    carry = win_step(t0 - T - 2 * C, _online_init(R), None)
    carry = win_step(t0 - T - C, carry, None)
    o_w = _online_finish(win_step(t0 - T, carry, tdiag))

    gates = gate_ref[0, 0]
    for h in range(hg):
        rows = slice(h * T, (h + 1) * T)
        o = (gates[:, 3 * h:3 * h + 1] * o_c[rows] + gates[:, 3 * h + 1:3 * h + 2] * o_s[rows]
             + gates[:, 3 * h + 2:3 * h + 3] * o_w[rows])
        o_ref[0, h] = o.astype(o_ref.dtype)


def _nsa(q, kc, vc, ks, vs, kw, vw, gates, bias_c, tdiag, overlap):
    b, _, s, _ = q.shape
    g, hg = NSA_KV_HEADS, NSA_GROUP
    nq = s // Q_BLOCK
    n_cp = kc.shape[2]
    full = lambda rows: pl.BlockSpec((1, 1, rows, HEAD_DIM), lambda bi, gi, qi: (bi, gi, 0, 0))
    return pl.pallas_call(
        _nsa_kernel,
        grid=(b, g, nq),
        in_specs=[
            pl.BlockSpec((1, hg, Q_BLOCK, HEAD_DIM), lambda bi, gi, qi: (bi, gi, qi, 0)),
            full(n_cp), full(n_cp), full(s), full(s), full(s), full(s),
            pl.BlockSpec((1, 1, Q_BLOCK, hg * 3), lambda bi, gi, qi: (bi, gi, qi, 0)),
            pl.BlockSpec((1, hg, Q_BLOCK, n_cp), lambda bi, gi, qi: (qi, gi, 0, 0)),
            pl.BlockSpec((1, hg, Q_BLOCK, KEY_CHUNK), lambda bi, gi, qi: (jnp.minimum(qi, 1), gi, 0, 0)),
            _const_spec(overlap.shape),
        ],
        out_specs=pl.BlockSpec((1, hg, Q_BLOCK, HEAD_DIM), lambda bi, gi, qi: (bi, gi, qi, 0)),
        out_shape=jax.ShapeDtypeStruct((b, NSA_HEADS, s, HEAD_DIM), BF16),
        compiler_params=_params("parallel", "parallel", "arbitrary"),
        name="nsa",
    )(q, kc, vc, ks, vs, kw, vw, gates, bias_c, tdiag, overlap)


def _sortable_key(x):
    b = lax.bitcast_convert_type(jnp.where(x == 0.0, 0.0, x), I32)
    return b ^ ((b >> 31) & 0x7FFFFFFF)


def _dsa_kernel(q_ref, iq_ref, iw_ref, ik_ref, k_ref, v_ref, tdiag_ref, o_ref, key_ref, *, n_keep):
    qi = pl.program_id(1)
    H, T, C = DSA_HEADS, Q_BLOCK, KEY_CHUNK
    R = H * T
    t0 = qi * T
    n_chunks = qi // 2 + 1
    q = q_ref[0].reshape(R, HEAD_DIM)
    iq = iq_ref[0].reshape(IDX_HEADS * T, IDX_DIM)
    iw = iw_ref[0]
    col = lax.broadcasted_iota(I32, (T, C), 1)
    tpos = t0 + lax.broadcasted_iota(I32, (T, C), 0)
    int_min = jnp.iinfo(jnp.int32).min

    def score_step(j, _):
        start = pl.multiple_of(j * C, C)
        act = jnp.maximum(_dot_nt(iq, ik_ref[0, pl.ds(start, C), :]), 0.0)
        score = act[0:T] * iw[:, 0:1]
        for h in range(1, IDX_HEADS):
            score = score + act[h * T:(h + 1) * T] * iw[:, h:h + 1]
        score = jnp.where(col + start <= tpos, score, NEG)
        key_ref[:, pl.ds(start, C)] = _sortable_key(score)
        return 0

    lax.fori_loop(0, n_chunks, score_step, 0)

    def count(pred_fn):
        def body(j, acc):
            start = pl.multiple_of(j * C, C)
            return acc + jnp.where(pred_fn(key_ref[:, pl.ds(start, C)], start), 1, 0)
        acc = lax.fori_loop(0, n_chunks, body, jnp.zeros((T, C), I32))
        return jnp.sum(acc.astype(F32), axis=1, keepdims=True).astype(I32)

    def thr_bit(i, thr_u):
        cand_u = thr_u | lax.shift_left(jnp.int32(1), 31 - i)
        cand_b = jnp.broadcast_to(cand_u ^ int_min, (T, C))
        n_ge = count(lambda keys, _: keys >= cand_b)
        return jnp.where(n_ge >= n_keep, cand_u, thr_u)

    thr = lax.fori_loop(0, 32, thr_bit, jnp.zeros((T, 1), I32)) ^ int_min
    thr_b = jnp.broadcast_to(thr, (T, C))

    need = n_keep - count(lambda keys, _: keys > thr_b)
    n_pos_bits = int(math.ceil(math.log2(key_ref.shape[1] + 1)))

    def pos_bit(i, lim):
        cand = lim + lax.shift_left(jnp.int32(1), n_pos_bits - 1 - i)
        cand_b = jnp.broadcast_to(cand, (T, C))
        n_tied = count(lambda keys, start: (keys == thr_b) & (col + start < cand_b))
        return jnp.where(n_tied <= need, cand, lim)

    lim = lax.fori_loop(0, n_pos_bits, pos_bit, jnp.zeros((T, 1), I32))
    lim_b = jnp.broadcast_to(lim, (T, C))

    def att_step(start, carry, table):
        keys = key_ref[:, pl.ds(start, C)]
        pos = col + start
        keep = (keys > thr_b) | ((keys == thr_b) & (pos < lim_b))
        s = _dot_nt(q, k_ref[0, pl.ds(start, C), :]).reshape(H, T, C)
        if table is None:
            keep = keep & (pos < t0 - T)
        else:
            s = s + table
        s = jnp.where(keep[None], s, NEG)
        return _online_update(carry, s.reshape(R, C), v_ref[0, pl.ds(start, C), :])

    carry = lax.fori_loop(0, qi // 2, lambda j, c: att_step(pl.multiple_of(j * C, C), c, None),
                          _online_init(R))
    diag_start = pl.multiple_of(jnp.maximum(t0 - T, 0), T)
    o = _online_finish(att_step(diag_start, carry, tdiag_ref[0]))
    o_ref[0] = o.reshape(H, T, HEAD_DIM).astype(o_ref.dtype)


def _dsa(q, iq, iw, ik, k, v, tdiag):
    b, _, s, _ = q.shape
    nq = s // Q_BLOCK
    n_keep = min(DSA_TOPK_MAX, s // 4)
    heads = lambda: pl.BlockSpec((1, DSA_HEADS, Q_BLOCK, HEAD_DIM), lambda bi, qi: (bi, 0, qi, 0))
    full = lambda: pl.BlockSpec((1, s, HEAD_DIM), lambda bi, qi: (bi, 0, 0))
    return pl.pallas_call(
        functools.partial(_dsa_kernel, n_keep=n_keep),
        grid=(b, nq),
        in_specs=[heads(), heads(),
                  pl.BlockSpec((1, Q_BLOCK, IDX_HEADS), lambda bi, qi: (bi, qi, 0)),
                  full(), full(), full(),
                  pl.BlockSpec((1, DSA_HEADS, Q_BLOCK, KEY_CHUNK), lambda bi, qi: (jnp.minimum(qi, 1), 1, 0, 0))],
        out_specs=heads(),
        out_shape=jax.ShapeDtypeStruct((b, DSA_HEADS, s, HEAD_DIM), BF16),
        scratch_shapes=[pltpu.VMEM((Q_BLOCK, s), I32)],
        compiler_params=_params("parallel", "arbitrary"),
        name="dsa",
    )(q, iq, iw, ik, k, v, tdiag)


def _merge_kernel(x_ref, oa_ref, ob_ref, ga_ref, gb_ref, wa_ref, wb_ref, wo_ref, gn_ref, wr_ref, br_ref,
                  x1_ref, xn_ref, gate_ref):
    def branch(o_ref, w_ref):
        acc = _dot(o_ref[0, 0], w_ref[0])
        for h in range(1, o_ref.shape[1]):
            acc = acc + _dot(o_ref[0, h], w_ref[h])
        return acc

    merged = ga_ref[...] * branch(oa_ref, wa_ref) + gb_ref[...] * branch(ob_ref, wb_ref)
    x1 = x_ref[...] + _dot(merged.astype(BF16), wo_ref[...])
    x1_ref[...] = x1
    xn = _row_rms(x1, gn_ref[...]).astype(BF16)
    xn_ref[...] = xn

    logits = _dot(xn, wr_ref[...]) + br_ref[...]
    lane = lax.broadcasted_iota(I32, logits.shape, 1)
    work = jnp.where(lane < N_EXPERTS, logits, -3e38)
    gates = jnp.zeros_like(logits)
    denom = jnp.zeros((logits.shape[0], 1), F32)
    top = None
    for _ in range(MOE_TOPK):
        mx = jnp.max(work, axis=1, keepdims=True)
        first = jnp.min(jnp.where(work == mx, lane, LANES), axis=1, keepdims=True)
        hit = lane == first
        top = mx if top is None else top
        e = jnp.exp(mx - top)
        gates = jnp.where(hit, e, gates)
        denom = denom + e
        work = jnp.where(hit, -3e38, work)
    gate_ref[...] = gates / denom


def _merge(x, oa, ob, ga, gb, w_branch_a, w_branch_b, w_out, ffn_norm, w_router, b_router, tm):
    b, s, d = x.shape
    n = b * s
    xf = x.reshape(n, d)
    nt = s // tm
    wa = w_branch_a.reshape(NSA_HEADS, HEAD_DIM, d).astype(BF16)
    wb = w_branch_b.reshape(DSA_HEADS, HEAD_DIM, d).astype(BF16)
    wr = jnp.concatenate([w_router, jnp.zeros((d, LANES - N_EXPERTS), F32)], axis=1).astype(BF16)
    br = jnp.concatenate([b_router, jnp.zeros((LANES - N_EXPERTS,), F32)]).reshape(1, LANES)
    consts = [wa, wb, w_out.astype(BF16), ffn_norm.reshape(1, d), wr, br]
    row = lambda w: pl.BlockSpec((tm, w), lambda bi, ti: (bi * nt + ti, 0))
    heads = lambda: pl.BlockSpec((1, NSA_HEADS, tm, HEAD_DIM), lambda bi, ti: (bi, 0, ti, 0))
    return pl.pallas_call(
        _merge_kernel,
        grid=(b, nt),
        in_specs=[row(d), heads(), heads(), row(d), row(d)] + [_const_spec(c.shape) for c in consts],
        out_specs=[row(d), row(d), row(LANES)],
        out_shape=[jax.ShapeDtypeStruct((n, d), F32), jax.ShapeDtypeStruct((n, d), BF16),
                   jax.ShapeDtypeStruct((n, LANES), F32)],
        compiler_params=_params("parallel", "parallel"),
        name="merge",
    )(xf, oa, ob, ga, gb, *consts)


def _moe_kernel(x1_ref, xn_ref, gate_ref, wg_ref, wl_ref, bg_ref, bl_ref, wd_ref, bd_ref, o_ref):
    e = pl.program_id(1)

    @pl.when(e == 0)
    def _():
        o_ref[...] = x1_ref[...]

    xn = xn_ref[...]
    glu = jnp.minimum(_dot(xn, wg_ref[0]) + bg_ref[0], SWIGLU_LIMIT)
    lin = jnp.clip(_dot(xn, wl_ref[0]) + bl_ref[0], -SWIGLU_LIMIT, SWIGLU_LIMIT)
    act = glu * jax.nn.sigmoid(SWIGLU_ALPHA * glu) * (lin + 1.0)
    y = _dot(act.astype(BF16), wd_ref[0]) + bd_ref[0]
    gates = gate_ref[...]
    lane = lax.broadcasted_iota(I32, gates.shape, 1)
    g = jnp.sum(jnp.where(lane == e, gates, 0.0), axis=1, keepdims=True)
    o_ref[...] += y * g


def _moe(x1, xn, gates, w_gate_up, b_gate_up, w_down, b_down, tm):
    n, d = x1.shape
    f = D_EXPERT
    wg = w_gate_up[:, :, 0::2].astype(BF16)
    wl = w_gate_up[:, :, 1::2].astype(BF16)
    bg = b_gate_up[:, 0::2].reshape(N_EXPERTS, 1, f)
    bl = b_gate_up[:, 1::2].reshape(N_EXPERTS, 1, f)
    wd = w_down.astype(BF16)
    bd = b_down.reshape(N_EXPERTS, 1, d)
    row = lambda w: pl.BlockSpec((tm, w), lambda i, e: (i, 0))
    per_e = lambda a, b_: pl.BlockSpec((1, a, b_), lambda i, e: (e, 0, 0))
    return pl.pallas_call(
        _moe_kernel,
        grid=(n // tm, N_EXPERTS),
        in_specs=[row(d), row(d), row(LANES), per_e(d, f), per_e(d, f), per_e(1, f), per_e(1, f),
                  per_e(f, d), per_e(1, d)],
        out_specs=row(d),
        out_shape=jax.ShapeDtypeStruct((n, d), F32),
        compiler_params=_params("parallel", "arbitrary"),
        name="moe",
    )(x1, xn, gates, wg, wl, bg, bl, wd, bd)


def _rel_bucket(rel):
    n = jnp.maximum(rel, 0)
    exact = REL_BUCKETS // 2
    nf = jnp.maximum(n, 1).astype(F32)
    large = exact + (jnp.log(nf / exact) / math.log(REL_MAX_DIST / exact) * (REL_BUCKETS - exact)).astype(I32)
    large = jnp.minimum(large, REL_BUCKETS - 1)
    return jnp.where(n < exact, n, large)


def _bias_tables(rel_bias, s):
    i = np.arange(Q_BLOCK)[:, None]
    j = np.arange(KEY_CHUNK)[None, :]
    rel = jnp.asarray(np.stack([i - j, i - j + Q_BLOCK]))
    far = rel_bias[REL_BUCKETS - 1]
    tdiag = jnp.where(rel[..., None] >= 0, rel_bias[_rel_bucket(rel)] - far, NEG)
    tdiag = tdiag.transpose(0, 3, 1, 2)
    n_cp = s // CMP_STRIDE
    t = np.arange(s)[:, None]
    c_last = np.arange(n_cp)[None, :] * CMP_STRIDE + (CMP_LEN - 1)
    rel_c = jnp.asarray(t - c_last)
    bias_c = jnp.where(rel_c[..., None] >= 0, rel_bias[:, :NSA_HEADS][_rel_bucket(rel_c)], NEG)
    bias_c = bias_c.reshape(s // Q_BLOCK, Q_BLOCK, n_cp, NSA_HEADS).transpose(0, 3, 1, 2)
    return tdiag.astype(F32), bias_c.astype(F32)


def _overlap_matrix(s):
    n_cp = s // CMP_STRIDE
    n_s = s // SLC_LEN
    c_start = np.arange(n_cp) * CMP_STRIDE
    s_start = np.arange(n_s) * SLC_LEN
    ovl = (c_start[:, None] < s_start[None, :] + SLC_LEN) & (c_start[:, None] + CMP_LEN > s_start[None, :])
    ovl[n_cp - 1] = False
    return jnp.asarray(ovl.astype(np.float32), BF16)


def _layer(x, attn_norm, w_in, nsa_q_norm, nsa_k_norm, cmp_k_pe, cmp_k_w1, cmp_k_b1, cmp_k_w2,
           cmp_v_pe, cmp_v_w1, cmp_v_b1, cmp_v_w2, dsa_kv_norm, dsa_w_uk, dsa_w_uv, dsa_q_norm,
           dsa_k_norm, w_branch_a, w_branch_b, w_out, rel_bias, ffn_norm, w_router, b_router,
           w_gate_up, b_gate_up, w_down, b_down):
    b, s, d = x.shape
    n = b * s
    g = NSA_KV_HEADS
    (qa, ck, cv, ks, sv, kw, wv, misc, qb, kb, vb, iq, ik, ga, gb) = _in_proj(
        x.reshape(n, d), attn_norm, w_in, nsa_q_norm, nsa_k_norm, dsa_q_norm, dsa_kv_norm, dsa_w_uk,
        dsa_w_uv, dsa_k_norm, tm=min(256, s))

    heads_first = lambda a, h: a.reshape(b, s, h, HEAD_DIM).transpose(0, 2, 1, 3)
    chunks = lambda a: heads_first(a, g).reshape(b, g, s // CMP_STRIDE, CMP_STRIDE * HEAD_DIM)
    kc = _compress(chunks(ck), cmp_k_pe, cmp_k_w1, cmp_k_b1, cmp_k_w2, nsa_k_norm, True)
    vc = _compress(chunks(cv), cmp_v_pe, cmp_v_w1, cmp_v_b1, cmp_v_w2, nsa_k_norm, False)

    tdiag, bias_c = _bias_tables(rel_bias, s)
    gates = misc[:, :NSA_HEADS * 3].reshape(b, s, g, NSA_GROUP * 3).transpose(0, 2, 1, 3)
    oa = _nsa(heads_first(qa, NSA_HEADS), kc, vc, heads_first(ks, g), heads_first(sv, g),
              heads_first(kw, g), heads_first(wv, g), gates, bias_c, tdiag, _overlap_matrix(s))

    iw = misc[:, NSA_HEADS * 3:NSA_HEADS * 3 + IDX_HEADS].reshape(b, s, IDX_HEADS)
    ob = _dsa(heads_first(qb, DSA_HEADS), heads_first(iq, IDX_HEADS), iw, ik.reshape(b, s, IDX_DIM),
              kb.reshape(b, s, HEAD_DIM), vb.reshape(b, s, HEAD_DIM), tdiag)

    x1, xn, moe_gates = _merge(x, oa, ob, ga, gb, w_branch_a, w_branch_b, w_out, ffn_norm, w_router,
                               b_router, tm=min(512, s))
    out = _moe(x1, xn, moe_gates, w_gate_up, b_gate_up, w_down, b_down, tm=min(512, n))
    return out.reshape(b, s, d)


def kernel(x, attn_norm, w_in, nsa_q_norm, nsa_k_norm, cmp_k_pe, cmp_k_w1, cmp_k_b1, cmp_k_w2, cmp_v_pe, cmp_v_w1, cmp_v_b1, cmp_v_w2, dsa_kv_norm, dsa_w_uk, dsa_w_uv, dsa_q_norm, dsa_k_norm, w_branch_a, w_branch_b, w_out, rel_bias, ffn_norm, w_router, b_router, w_gate_up, b_gate_up, w_down, b_down):
    for l in range(attn_norm.shape[0]):
        x = _layer(x, attn_norm[l], w_in[l], nsa_q_norm[l], nsa_k_norm[l], cmp_k_pe[l], cmp_k_w1[l],
                   cmp_k_b1[l], cmp_k_w2[l], cmp_v_pe[l], cmp_v_w1[l], cmp_v_b1[l], cmp_v_w2[l],
                   dsa_kv_norm[l], dsa_w_uk[l], dsa_w_uv[l], dsa_q_norm[l], dsa_k_norm[l],
                   w_branch_a[l], w_branch_b[l], w_out[l], rel_bias, ffn_norm[l], w_router[l],
                   b_router[l], w_gate_up[l], b_gate_up[l], w_down[l], b_down[l])
    return x
```

```python
import functools
import math

import numpy as np
import jax
import jax.numpy as jnp
from jax import lax
from jax.experimental import pallas as pl
from jax.experimental.pallas import tpu as pltpu

D_MODEL = 1024
HEAD_DIM = 64
NSA_HEADS = 8
NSA_KV_HEADS = 2
NSA_GROUP = NSA_HEADS // NSA_KV_HEADS
CMP_LEN = 32
CMP_STRIDE = 16
CMP_HID = 128
SLC_LEN = 64
SLC_TOPN = 16
WINDOW = 512
DSA_HEADS = 8
DSA_KV_RANK = 128
IDX_HEADS = 8
IDX_DIM = 64
DSA_TOPK_MAX = 256
N_EXPERTS = 32
MOE_TOPK = 4
D_EXPERT = 1024
SWIGLU_LIMIT = 7.0
SWIGLU_ALPHA = 1.702
REL_BUCKETS = 32
REL_MAX_DIST = 128
NORM_EPS = 1e-6
NEG = -1e30

Q_BLOCK = 128
KEY_CHUNK = 256
CMP_BAND = 24
COUNT_ROWS = 32
LANES = 128
SUBLANES = 8
MXU_DIM = 256
VMEM_LIMIT = 56 * 1024 * 1024

F32 = jnp.float32
BF16 = jnp.bfloat16
I32 = jnp.int32


def _params(*sem):
    return pltpu.CompilerParams(dimension_semantics=sem, vmem_limit_bytes=VMEM_LIMIT)


def _dot(a, b):
    return jnp.dot(a, b, preferred_element_type=F32)


def _dot_nt(a, b):
    return lax.dot_general(a, b, (((1,), (1,)), ((), ())), preferred_element_type=F32)


def _const_spec(shape):
    nd = len(shape)
    return pl.BlockSpec(shape, lambda *_: (0,) * nd)


def _segment_sumsq(y, seg_ones):
    y2 = y * y
    hi = y2.astype(BF16)
    lo = (y2 - hi.astype(F32)).astype(BF16)
    return _dot(hi, seg_ones) + _dot(lo, seg_ones)


def _head_rms(y, seg_ones, gain):
    ms = _segment_sumsq(y, seg_ones) * (1.0 / HEAD_DIM)
    return y * lax.rsqrt(ms + NORM_EPS) * gain


def _row_rms(y, gain):
    ms = jnp.mean(y * y, axis=-1, keepdims=True)
    return y * lax.rsqrt(ms + NORM_EPS) * gain


def _in_proj_kernel(x_ref, g_ref, wqa_ref, wkv_ref, wmisc_ref, wqb_ref, wckv_ref, wiq_ref, wik_ref,
                    wga_ref, wgb_ref, seg512_ref, seg128_ref, gqa_ref, gka_ref, gqb_ref, gckv_ref,
                    wuk_ref, wuv_ref, gkb_ref,
                    qa_ref, ck_ref, cv_ref, ks_ref, sv_ref, kw_ref, wv_ref, misc_ref, qb_ref,
                    kb_ref, vb_ref, iq_ref, ik_ref, ga_ref, gb_ref):
    h = _row_rms(x_ref[...], g_ref[...]).astype(BF16)
    scale = HEAD_DIM ** -0.5
    seg512 = seg512_ref[...]
    seg128 = seg128_ref[...]

    qa = _head_rms(_dot(h, wqa_ref[...]), seg512, gqa_ref[...])
    qa_ref[...] = (qa * scale).astype(BF16)

    kv = _dot(h, wkv_ref[...])
    ck_ref[...] = kv[:, 0:128]
    cv_ref[...] = kv[:, 128:256]
    ks_ref[...] = _head_rms(kv[:, 256:384], seg128, gka_ref[...]).astype(BF16)
    sv_ref[...] = kv[:, 384:512].astype(BF16)
    kw_ref[...] = _head_rms(kv[:, 512:640], seg128, gka_ref[...]).astype(BF16)
    wv_ref[...] = kv[:, 640:768].astype(BF16)

    misc = _dot(h, wmisc_ref[...])
    lane = lax.broadcasted_iota(I32, misc.shape, 1)
    iw_scale = (IDX_HEADS ** -0.5)
    idx_scale = IDX_DIM ** -0.5
    misc_ref[...] = jnp.where(lane < NSA_HEADS * 3, jax.nn.sigmoid(misc), (misc * iw_scale) * idx_scale)

    qb = _head_rms(_dot(h, wqb_ref[...]), seg512, gqb_ref[...])
    qb_ref[...] = (qb * scale).astype(BF16)

    ckv = _row_rms(_dot(h, wckv_ref[...]), gckv_ref[...]).astype(BF16)
    kb_ref[...] = _row_rms(_dot(ckv, wuk_ref[...]), gkb_ref[...]).astype(BF16)
    vb_ref[...] = _dot(ckv, wuv_ref[...]).astype(BF16)

    iq_ref[...] = _dot(h, wiq_ref[...]).astype(BF16)
    ik_ref[...] = _dot(h, wik_ref[...]).astype(BF16)
    ga_ref[...] = jax.nn.sigmoid(_dot(h, wga_ref[...]))
    gb_ref[...] = jax.nn.sigmoid(_dot(h, wgb_ref[...]))


def _in_proj(xf, attn_norm, w_in, nsa_q_norm, nsa_k_norm, dsa_q_norm, dsa_kv_norm, dsa_w_uk, dsa_w_uv,
             dsa_k_norm, tm):
    n = xf.shape[0]
    kvw = NSA_KV_HEADS * HEAD_DIM
    widths = (NSA_HEADS * HEAD_DIM, kvw, kvw, kvw, kvw, kvw, kvw, NSA_HEADS * 3,
              DSA_HEADS * HEAD_DIM, DSA_KV_RANK, IDX_HEADS * IDX_DIM, IDX_DIM, IDX_HEADS, D_MODEL, D_MODEL)
    offs = np.concatenate([[0], np.cumsum(widths)])
    sec = [w_in[:, offs[i]:offs[i + 1]] for i in range(len(widths))]
    wb = lambda w: w.astype(BF16)
    wqa = wb(sec[0])
    wkv = wb(jnp.concatenate(sec[1:7], axis=1))
    wmisc = wb(jnp.concatenate([sec[7], sec[12], jnp.zeros((D_MODEL, LANES - 32), F32)], axis=1))
    wqb, wckv, wiq, wik, wga, wgb = wb(sec[8]), wb(sec[9]), wb(sec[10]), wb(sec[11]), wb(sec[13]), wb(sec[14])

    def seg_ones(w):
        i = np.arange(w) // HEAD_DIM
        return jnp.asarray((i[:, None] == i[None, :]).astype(np.float32), BF16)

    consts = [attn_norm.reshape(1, -1), wqa, wkv, wmisc, wqb, wckv, wiq, wik, wga, wgb,
              seg_ones(512), seg_ones(128),
              jnp.tile(nsa_q_norm, NSA_HEADS).reshape(1, -1), jnp.tile(nsa_k_norm, NSA_KV_HEADS).reshape(1, -1),
              jnp.tile(dsa_q_norm, DSA_HEADS).reshape(1, -1), dsa_kv_norm.reshape(1, -1),
              wb(dsa_w_uk), wb(dsa_w_uv), dsa_k_norm.reshape(1, -1)]
    out_defs = [(512, BF16), (128, F32), (128, F32), (128, BF16), (128, BF16), (128, BF16), (128, BF16),
                (128, F32), (512, BF16), (64, BF16), (64, BF16), (512, BF16), (64, BF16),
                (D_MODEL, F32), (D_MODEL, F32)]
    row = lambda w: pl.BlockSpec((tm, w), lambda i: (i, 0))
    return pl.pallas_call(
        _in_proj_kernel,
        grid=(n // tm,),
        in_specs=[row(D_MODEL)] + [_const_spec(c.shape) for c in consts],
        out_specs=[row(w) for w, _ in out_defs],
        out_shape=[jax.ShapeDtypeStruct((n, w), dt) for w, dt in out_defs],
        compiler_params=_params("parallel"),
        name="in_proj",
    )(xf, *consts)


def _compress_kernel(c_ref, pea_ref, peb_ref, w1a_ref, w1b_ref, b1_ref, w2_ref, gain_ref, o_ref, *, norm, n_c):
    c = c_ref[0, 0]
    ua = _dot((c + pea_ref[...]).astype(BF16), w1a_ref[...])
    ub = _dot((c + peb_ref[...]).astype(BF16), w1b_ref[...])
    rows = c.shape[0]
    ub_next = pltpu.roll(ub, rows - 1, 0)
    hid = jax.nn.gelu(ua + ub_next + b1_ref[...])
    o = _dot(hid.astype(BF16), w2_ref[...])
    if norm:
        o = _row_rms(o, gain_ref[...])
    ridx = lax.broadcasted_iota(I32, o.shape, 0)
    o_ref[0, 0] = jnp.where(ridx < n_c, o, 0.0).astype(o_ref.dtype)


def _compress(c, pe, w1, b1, w2, gain, norm):
    b, g, rows, width = c.shape
    half = CMP_STRIDE * HEAD_DIM
    consts = [pe[:CMP_STRIDE].reshape(1, half), pe[CMP_STRIDE:].reshape(1, half),
              w1[:half].astype(BF16), w1[half:].astype(BF16), b1.reshape(1, -1), w2.astype(BF16),
              gain.reshape(1, -1)]
    return pl.pallas_call(
        functools.partial(_compress_kernel, norm=norm, n_c=rows - 1),
        grid=(b, g),
        in_specs=[pl.BlockSpec((1, 1, rows, width), lambda i, j: (i, j, 0, 0))] + [_const_spec(x.shape) for x in consts],
        out_specs=pl.BlockSpec((1, 1, rows, HEAD_DIM), lambda i, j: (i, j, 0, 0)),
        out_shape=jax.ShapeDtypeStruct((b, g, rows, HEAD_DIM), BF16),
        compiler_params=_params("parallel", "parallel"),
        name="compress_norm" if norm else "compress",
    )(c, *consts)


def _attend(carry, q_ref, k, vt, add, table, p_ref):
    ms, ls, acc = carry
    T = Q_BLOCK
    new_m, new_l, alphas = [], [], []
    for h in range(len(ms)):
        s = _dot_nt(k, q_ref[0, h]) + add
        if table is not None:
            s = s + table[:, h * T:(h + 1) * T]
        m_new = jnp.maximum(ms[h], jnp.max(s, axis=0, keepdims=True))
        alpha = jnp.exp(ms[h] - m_new)
        p = jnp.exp(s - m_new)
        new_l.append(alpha * ls[h] + jnp.sum(p, axis=0, keepdims=True))
        new_m.append(m_new)
        alphas.append(alpha)
        p_ref[:, h * T:(h + 1) * T] = p.astype(p_ref.dtype)
    acc = acc * jnp.concatenate(alphas, axis=1) + _dot(vt, p_ref[...])
    return new_m, new_l, acc


def _attend_init(heads):
    return ([jnp.full((1, Q_BLOCK), NEG, F32) for _ in range(heads)],
            [jnp.zeros((1, Q_BLOCK), F32) for _ in range(heads)],
            jnp.zeros((HEAD_DIM, heads * Q_BLOCK), F32))


def _attend_finish(carry):
    _, ls, acc = carry
    return acc / jnp.concatenate(ls, axis=1)


def _nsa_kernel(q_ref, kc_ref, vct_ref, ks_ref, vst_ref, kw_ref, vwt_ref, gate_ref, band_ref, tdiag_ref,
                ovlt_ref, o_ref, sc_ref, p_ref):
    qi = pl.program_id(2)
    hg, T, C = NSA_GROUP, Q_BLOCK, KEY_CHUNK
    t0 = qi * T
    n_cp = kc_ref.shape[2]
    n_s = ovlt_ref.shape[0]
    row_c = lax.broadcasted_iota(I32, (C, T), 0)
    tpos_c = t0 + lax.broadcasted_iota(I32, (C, T), 1)
    diag_start = pl.multiple_of(jnp.maximum(t0 - T, 0), T)

    band0 = pl.multiple_of(jnp.maximum(t0 // CMP_STRIDE - 2 * SUBLANES, 0), SUBLANES)
    past = jnp.where(lax.broadcasted_iota(I32, (n_cp, T), 0) >= band0 + CMP_BAND, NEG, 0.0)
    kc = kc_ref[0, 0]
    vct = vct_ref[0, 0]
    ovlt = ovlt_ref[...]
    imp = jnp.zeros((n_s, T), F32)
    o_c = []
    for h in range(hg):
        sc_ref[...] = _dot_nt(kc, q_ref[0, h]) + past
        sc_ref[pl.ds(band0, CMP_BAND), :] += band_ref[0, :, h * T:(h + 1) * T]
        s = sc_ref[...]
        m = jnp.max(s, axis=0, keepdims=True)
        p = jnp.where(s > 0.5 * NEG, jnp.exp(s - m), 0.0)
        l = jnp.sum(p, axis=0, keepdims=True)
        pb = (p / jnp.where(l > 0.0, l, 1.0)).astype(BF16)
        o_c.append(_dot(vct, pb))
        imp = imp + _dot(ovlt, pb)

    blk = lax.broadcasted_iota(I32, (n_s, T), 0)
    blk_f = blk.astype(F32)
    cur = (t0 + lax.broadcasted_iota(I32, (n_s, T), 1)) // SLC_LEN
    forced = (blk == 0) | (blk == cur) | (blk == cur - 1)
    work = jnp.where(forced, 1e6, jnp.where(blk <= cur, imp, -1e6))

    def pick(_, carry):
        work, sel = carry
        mx = jnp.max(work, axis=0, keepdims=True)
        first = jnp.min(jnp.where(work == mx, blk_f, float(n_s)), axis=0, keepdims=True)
        hit = blk_f == first
        return jnp.where(hit, -3e38, work), jnp.where(hit, 1.0, sel)

    _, sel = lax.fori_loop(0, min(SLC_TOPN, n_s), pick, (work, jnp.zeros((n_s, T), F32)))
    sel_b = sel.astype(BF16)
    x_row = lax.broadcasted_iota(I32, (C, n_s), 0)
    x_blk = lax.broadcasted_iota(I32, (C, n_s), 1)

    def sel_step(start, carry, far):
        expand = jnp.where(x_blk == (x_row + start) // SLC_LEN, 1.0, 0.0).astype(BF16)
        add = jnp.where(_dot(expand, sel_b) > 0.5, 0.0, NEG)
        if far:
            add = add + jnp.where(row_c + start < t0 - T, 0.0, NEG)
        return _attend(carry, q_ref, ks_ref[0, 0, pl.ds(start, C), :], vst_ref[0, 0, :, pl.ds(start, C)],
                       add, None if far else tdiag_ref[0], p_ref)

    carry = lax.fori_loop(0, qi // 2, lambda j, c: sel_step(pl.multiple_of(j * C, C), c, True),
                          _attend_init(hg))
    o_s = _attend_finish(sel_step(diag_start, carry, False))

    def win_step(lo, carry, last):
        start = pl.multiple_of(jnp.maximum(lo, 0), T)
        pos = row_c + start
        rel = tpos_c - pos
        add = jnp.where((pos < lo + C) & (rel >= 0) & (rel < WINDOW), 0.0, NEG)
        return _attend(carry, q_ref, kw_ref[0, 0, pl.ds(start, C), :], vwt_ref[0, 0, :, pl.ds(start, C)],
                       add, tdiag_ref[0] if last else None, p_ref)

    carry = win_step(t0 - T - 2 * C, _attend_init(hg), False)
    carry = win_step(t0 - T - C, carry, False)
    o_w = _attend_finish(win_step(t0 - T, carry, True))

    gates = gate_ref[0, 0]
    for h in range(hg):
        cols = slice(h * T, (h + 1) * T)
        o = (gates[3 * h:3 * h + 1] * o_c[h] + gates[3 * h + 1:3 * h + 2] * o_s[:, cols]
             + gates[3 * h + 2:3 * h + 3] * o_w[:, cols])
        o_ref[0, h] = o.T.astype(o_ref.dtype)


def _nsa(q, kc, vct, ks, vst, kw, vwt, gates_t, band, tdiag, overlap_t):
    b, _, s, _ = q.shape
    g, hg = NSA_KV_HEADS, NSA_GROUP
    nq = s // Q_BLOCK
    n_cp = kc.shape[2]
    rows = lambda r: pl.BlockSpec((1, 1, r, HEAD_DIM), lambda bi, gi, qi: (bi, gi, 0, 0))
    cols = lambda c: pl.BlockSpec((1, 1, HEAD_DIM, c), lambda bi, gi, qi: (bi, gi, 0, 0))
    return pl.pallas_call(
        _nsa_kernel,
        grid=(b, g, nq),
        in_specs=[
            pl.BlockSpec((1, hg, Q_BLOCK, HEAD_DIM), lambda bi, gi, qi: (bi, gi, qi, 0)),
            rows(n_cp), cols(n_cp), rows(s), cols(s), rows(s), cols(s),
            pl.BlockSpec((1, 1, hg * 3, Q_BLOCK), lambda bi, gi, qi: (bi, gi, 0, qi)),
            pl.BlockSpec((1, CMP_BAND, hg * Q_BLOCK), lambda bi, gi, qi: (jnp.minimum(qi, 2), 0, gi)),
            pl.BlockSpec((1, KEY_CHUNK, hg * Q_BLOCK), lambda bi, gi, qi: (jnp.minimum(qi, 1), 0, gi)),
            _const_spec(overlap_t.shape),
        ],
        out_specs=pl.BlockSpec((1, hg, Q_BLOCK, HEAD_DIM), lambda bi, gi, qi: (bi, gi, qi, 0)),
        out_shape=jax.ShapeDtypeStruct((b, NSA_HEADS, s, HEAD_DIM), BF16),
        scratch_shapes=[pltpu.VMEM((n_cp, Q_BLOCK), F32), pltpu.VMEM((KEY_CHUNK, hg * Q_BLOCK), BF16)],
        compiler_params=_params("parallel", "parallel", "arbitrary"),
        name="nsa",
    )(q, kc, vct, ks, vst, kw, vwt, gates_t, band, tdiag, overlap_t)


def _sortable_key(x):
    b = lax.bitcast_convert_type(jnp.where(x == 0.0, 0.0, x), I32)
    return b ^ ((b >> 31) & 0x7FFFFFFF)


def _dsa_kernel(q_ref, iq_ref, iw_ref, ik_ref, k_ref, vt_ref, tdiag_ref, o_ref, key_ref, lim_ref, p_ref,
                *, n_keep):
    qi = pl.program_id(1)
    H, T, C = DSA_HEADS, Q_BLOCK, KEY_CHUNK
    t0 = qi * T
    n_chunks = qi // 2 + 1
    iw = iw_ref[0]
    row = lax.broadcasted_iota(I32, (C, T), 0)
    tpos = t0 + lax.broadcasted_iota(I32, (C, T), 1)
    int_min = jnp.iinfo(jnp.int32).min

    def score_step(j, _):
        start = pl.multiple_of(j * C, C)
        ikc = ik_ref[0, pl.ds(start, C), :]
        score = jnp.maximum(_dot_nt(ikc, iq_ref[0, 0]), 0.0) * iw[0:1]
        for h in range(1, IDX_HEADS):
            score = score + jnp.maximum(_dot_nt(ikc, iq_ref[0, h]), 0.0) * iw[h:h + 1]
        score = jnp.where(row + start <= tpos, score, NEG)
        key_ref[pl.ds(start, C), :] = _sortable_key(score)
        return 0

    lax.fori_loop(0, n_chunks, score_step, 0)

    def count(pred_fn):
        def body(j, acc):
            start = pl.multiple_of(j * C, C)
            hits = jnp.where(pred_fn(key_ref[pl.ds(start, C), :], start), 1.0, 0.0)
            return acc + jnp.sum(hits.reshape(C // COUNT_ROWS, COUNT_ROWS, T), axis=0)
        acc = lax.fori_loop(0, n_chunks, body, jnp.zeros((COUNT_ROWS, T), F32))
        return jnp.sum(acc, axis=0, keepdims=True).astype(I32)

    def thr_bit(i, carry):
        thr_u, n_at = carry
        cand_u = thr_u | lax.shift_left(jnp.int32(1), 31 - i)
        cand = cand_u ^ int_min
        n_ge = count(lambda keys, _: keys >= cand)
        ok = n_ge >= n_keep
        return jnp.where(ok, cand_u, thr_u), jnp.where(ok, n_ge, n_at)

    thr_u, n_at = lax.fori_loop(0, 32, thr_bit, (jnp.zeros((1, T), I32),
                                                 jnp.full((1, T), 1, I32) * (n_chunks * C)))
    thr = thr_u ^ int_min

    n_pos_bits = int(math.ceil(math.log2(key_ref.shape[0] + 1)))
    lim_ref[...] = jnp.full((1, T), 1 << n_pos_bits, I32)

    @pl.when(jnp.max((n_at - n_keep).astype(F32)) > 0.0)
    def _():
        need = n_keep - count(lambda keys, _: keys > thr)

        def pos_bit(i, lim):
            cand = lim + lax.shift_left(jnp.int32(1), n_pos_bits - 1 - i)
            n_tied = count(lambda keys, start: (keys == thr) & (row + start < cand))
            return jnp.where(n_tied <= need, cand, lim)

        lim_ref[...] = lax.fori_loop(0, n_pos_bits, pos_bit, jnp.zeros((1, T), I32))

    lim = lim_ref[...]

    def att_step(start, carry, far):
        keys = key_ref[pl.ds(start, C), :]
        pos = row + start
        keep = (keys > thr) | ((keys == thr) & (pos < lim))
        if far:
            keep = keep & (pos < t0 - T)
        add = jnp.where(keep, 0.0, NEG)
        return _attend(carry, q_ref, k_ref[0, pl.ds(start, C), :], vt_ref[0, :, pl.ds(start, C)],
                       add, None if far else tdiag_ref[0], p_ref)

    carry = lax.fori_loop(0, qi // 2, lambda j, c: att_step(pl.multiple_of(j * C, C), c, True),
                          _attend_init(H))
    diag_start = pl.multiple_of(jnp.maximum(t0 - T, 0), T)
    o = _attend_finish(att_step(diag_start, carry, False))
    for h in range(H):
        o_ref[0, h] = o[:, h * T:(h + 1) * T].T.astype(o_ref.dtype)


def _dsa(q, iq, iw_t, ik, k, vt, tdiag):
    b, _, s, _ = q.shape
    nq = s // Q_BLOCK
    n_keep = min(DSA_TOPK_MAX, s // 4)
    heads = lambda: pl.BlockSpec((1, DSA_HEADS, Q_BLOCK, HEAD_DIM), lambda bi, qi: (bi, 0, qi, 0))
    rows = lambda: pl.BlockSpec((1, s, HEAD_DIM), lambda bi, qi: (bi, 0, 0))
    return pl.pallas_call(
        functools.partial(_dsa_kernel, n_keep=n_keep),
        grid=(b, nq),
        in_specs=[heads(), heads(),
                  pl.BlockSpec((1, IDX_HEADS, Q_BLOCK), lambda bi, qi: (bi, 0, qi)),
                  rows(), rows(),
                  pl.BlockSpec((1, HEAD_DIM, s), lambda bi, qi: (bi, 0, 0)),
                  pl.BlockSpec((1, KEY_CHUNK, DSA_HEADS * Q_BLOCK), lambda bi, qi: (jnp.minimum(qi, 1), 0, 1))],
        out_specs=heads(),
        out_shape=jax.ShapeDtypeStruct((b, DSA_HEADS, s, HEAD_DIM), BF16),
        scratch_shapes=[pltpu.VMEM((s, Q_BLOCK), I32), pltpu.VMEM((1, Q_BLOCK), I32),
                        pltpu.VMEM((KEY_CHUNK, DSA_HEADS * Q_BLOCK), BF16)],
        compiler_params=_params("parallel", "arbitrary"),
        name="dsa",
    )(q, iq, iw_t, ik, k, vt, tdiag)


def _merge_kernel(x_ref, oa_ref, ob_ref, ga_ref, gb_ref, wa_ref, wb_ref, wo_ref, gn_ref, wr_ref, br_ref,
                  x1_ref, xn_ref, gate_ref):
    def branch(o_ref, w_ref):
        acc = _dot(o_ref[0, 0], w_ref[0])
        for h in range(1, o_ref.shape[1]):
            acc = acc + _dot(o_ref[0, h], w_ref[h])
        return acc

    merged = ga_ref[...] * branch(oa_ref, wa_ref) + gb_ref[...] * branch(ob_ref, wb_ref)
    x1 = x_ref[...] + _dot(merged.astype(BF16), wo_ref[...])
    x1_ref[...] = x1
    xn = _row_rms(x1, gn_ref[...]).astype(BF16)
    xn_ref[...] = xn

    logits = _dot(xn, wr_ref[...]) + br_ref[...]
    lane = lax.broadcasted_iota(I32, logits.shape, 1)
    work = jnp.where(lane < N_EXPERTS, logits, -3e38)
    gates = jnp.zeros_like(logits)
    denom = jnp.zeros((logits.shape[0], 1), F32)
    top = None
    for _ in range(MOE_TOPK):
        mx = jnp.max(work, axis=1, keepdims=True)
        first = jnp.min(jnp.where(work == mx, lane, LANES), axis=1, keepdims=True)
        hit = lane == first
        top = mx if top is None else top
        e = jnp.exp(mx - top)
        gates = jnp.where(hit, e, gates)
        denom = denom + e
        work = jnp.where(hit, -3e38, work)
    gate_ref[...] = gates / denom


def _merge(x, oa, ob, ga, gb, w_branch_a, w_branch_b, w_out, ffn_norm, w_router, b_router, tm):
    b, s, d = x.shape
    n = b * s
    xf = x.reshape(n, d)
    nt = s // tm
    wa = w_branch_a.reshape(NSA_HEADS, HEAD_DIM, d).astype(BF16)
    wb = w_branch_b.reshape(DSA_HEADS, HEAD_DIM, d).astype(BF16)
    wr = jnp.concatenate([w_router, jnp.zeros((d, LANES - N_EXPERTS), F32)], axis=1).astype(BF16)
    br = jnp.concatenate([b_router, jnp.zeros((LANES - N_EXPERTS,), F32)]).reshape(1, LANES)
    consts = [wa, wb, w_out.astype(BF16), ffn_norm.reshape(1, d), wr, br]
    row = lambda w: pl.BlockSpec((tm, w), lambda bi, ti: (bi * nt + ti, 0))
    heads = lambda: pl.BlockSpec((1, NSA_HEADS, tm, HEAD_DIM), lambda bi, ti: (bi, 0, ti, 0))
    return pl.pallas_call(
        _merge_kernel,
        grid=(b, nt),
        in_specs=[row(d), heads(), heads(), row(d), row(d)] + [_const_spec(c.shape) for c in consts],
        out_specs=[row(d), row(d), row(LANES)],
        out_shape=[jax.ShapeDtypeStruct((n, d), F32), jax.ShapeDtypeStruct((n, d), BF16),
                   jax.ShapeDtypeStruct((n, LANES), F32)],
        compiler_params=_params("parallel", "parallel"),
        name="merge",
    )(xf, oa, ob, ga, gb, *consts)


def _deinterleave_index(width):
    half = MXU_DIM // 2
    c = np.arange(width)
    r = c % MXU_DIM
    return (c - r) + np.where(r < half, 2 * r, 2 * (r - half) + 1)


def _moe_prep_kernel(w_ref, perm_ref, o_ref):
    perm = perm_ref[...]
    for g in range(w_ref.shape[2] // MXU_DIM):
        cols = slice(g * MXU_DIM, (g + 1) * MXU_DIM)
        o_ref[0, :, cols] = _dot(w_ref[0, :, cols].astype(BF16), perm).astype(BF16)


def _moe_prep(w_gate_up, tr):
    e, d, f2 = w_gate_up.shape
    src = _deinterleave_index(MXU_DIM)
    perm = jnp.asarray((np.arange(MXU_DIM)[:, None] == src[None, :]).astype(np.float32), BF16)
    return pl.pallas_call(
        _moe_prep_kernel,
        grid=(e, d // tr),
        in_specs=[pl.BlockSpec((1, tr, f2), lambda i, j: (i, j, 0)), _const_spec(perm.shape)],
        out_specs=pl.BlockSpec((1, tr, f2), lambda i, j: (i, j, 0)),
        out_shape=jax.ShapeDtypeStruct((e, d, f2), BF16),
        compiler_params=_params("parallel", "parallel"),
        name="moe_prep",
    )(w_gate_up, perm)


def _moe_kernel(x1_ref, xn_ref, gate_ref, wgu_ref, bgu_ref, wd_ref, bd_ref, o_ref):
    e = pl.program_id(1)

    @pl.when(e == 0)
    def _():
        o_ref[...] = x1_ref[...]

    hu = _dot(xn_ref[...], wgu_ref[0]) + bgu_ref[0]
    half = MXU_DIM // 2
    acts = []
    for g in range(hu.shape[1] // MXU_DIM):
        glu = jnp.minimum(hu[:, g * MXU_DIM:g * MXU_DIM + half], SWIGLU_LIMIT)
        lin = jnp.clip(hu[:, g * MXU_DIM + half:(g + 1) * MXU_DIM], -SWIGLU_LIMIT, SWIGLU_LIMIT)
        acts.append((glu * jax.nn.sigmoid(SWIGLU_ALPHA * glu) * (lin + 1.0)).astype(BF16))
    y = _dot(jnp.concatenate(acts, axis=1), wd_ref[0]) + bd_ref[0]
    gates = gate_ref[...]
    lane = lax.broadcasted_iota(I32, gates.shape, 1)
    g_e = jnp.sum(jnp.where(lane == e, gates, 0.0), axis=1, keepdims=True)
    o_ref[...] += y * g_e


def _moe(x1, xn, gates, w_gate_up, b_gate_up, w_down, b_down, tm):
    n, d = x1.shape
    f = D_EXPERT
    wgu = _moe_prep(w_gate_up, tr=256)
    bgu = b_gate_up[:, _deinterleave_index(2 * f)].reshape(N_EXPERTS, 1, 2 * f)
    wd = w_down.astype(BF16)
    bd = b_down.reshape(N_EXPERTS, 1, d)
    row = lambda w: pl.BlockSpec((tm, w), lambda i, e: (i, 0))
    per_e = lambda a, b_: pl.BlockSpec((1, a, b_), lambda i, e: (e, 0, 0))
    return pl.pallas_call(
        _moe_kernel,
        grid=(n // tm, N_EXPERTS),
        in_specs=[row(d), row(d), row(LANES), per_e(d, 2 * f), per_e(1, 2 * f), per_e(f, d), per_e(1, d)],
        out_specs=row(d),
        out_shape=jax.ShapeDtypeStruct((n, d), F32),
        compiler_params=_params("parallel", "arbitrary"),
        name="moe",
    )(x1, xn, gates, wgu, bgu, wd, bd)


def _rel_bucket(rel):
    n = jnp.maximum(rel, 0)
    exact = REL_BUCKETS // 2
    nf = jnp.maximum(n, 1).astype(F32)
    large = exact + (jnp.log(nf / exact) / math.log(REL_MAX_DIST / exact) * (REL_BUCKETS - exact)).astype(I32)
    large = jnp.minimum(large, REL_BUCKETS - 1)
    return jnp.where(n < exact, n, large)


def _rel_table(rel_bias, rel):
    far = rel_bias[REL_BUCKETS - 1]
    tab = jnp.where(rel[..., None] >= 0, rel_bias[_rel_bucket(rel)] - far, NEG)
    tab = jnp.moveaxis(tab, -1, -2)
    return tab.reshape(tab.shape[:-2] + (-1,)).astype(F32)


def _bias_tables(rel_bias):
    i = np.arange(Q_BLOCK)[None, :]
    j = np.arange(KEY_CHUNK)[:, None]
    tdiag = _rel_table(rel_bias, jnp.asarray(np.stack([i - j, i - j + Q_BLOCK])))
    r = np.arange(CMP_BAND)[:, None]
    rel_band = np.stack([i - (CMP_LEN - 1) - CMP_STRIDE * (c0 + r) for c0 in (0, -SUBLANES, -2 * SUBLANES)])
    band = _rel_table(rel_bias[:, :NSA_HEADS], jnp.asarray(rel_band))
    return tdiag, band


def _overlap_matrix_t(s):
    n_cp = s // CMP_STRIDE
    n_s = s // SLC_LEN
    c_start = np.arange(n_cp) * CMP_STRIDE
    s_start = np.arange(n_s) * SLC_LEN
    ovl = (c_start[None, :] < s_start[:, None] + SLC_LEN) & (c_start[None, :] + CMP_LEN > s_start[:, None])
    ovl[:, n_cp - 1] = False
    return jnp.asarray(ovl.astype(np.float32), BF16)


def _layer(x, attn_norm, w_in, nsa_q_norm, nsa_k_norm, cmp_k_pe, cmp_k_w1, cmp_k_b1, cmp_k_w2,
           cmp_v_pe, cmp_v_w1, cmp_v_b1, cmp_v_w2, dsa_kv_norm, dsa_w_uk, dsa_w_uv, dsa_q_norm,
           dsa_k_norm, w_branch_a, w_branch_b, w_out, rel_bias, ffn_norm, w_router, b_router,
           w_gate_up, b_gate_up, w_down, b_down):
    b, s, d = x.shape
    n = b * s
    g = NSA_KV_HEADS
    (qa, ck, cv, ks, sv, kw, wv, misc, qb, kb, vb, iq, ik, ga, gb) = _in_proj(
        x.reshape(n, d), attn_norm, w_in, nsa_q_norm, nsa_k_norm, dsa_q_norm, dsa_kv_norm, dsa_w_uk,
        dsa_w_uv, dsa_k_norm, tm=min(256, s))

    heads_first = lambda a, h: a.reshape(b, s, h, HEAD_DIM).transpose(0, 2, 1, 3)
    heads_t = lambda a, h: a.reshape(b, s, h, HEAD_DIM).transpose(0, 2, 3, 1)
    chunks = lambda a: heads_first(a, g).reshape(b, g, s // CMP_STRIDE, CMP_STRIDE * HEAD_DIM)
    kc = _compress(chunks(ck), cmp_k_pe, cmp_k_w1, cmp_k_b1, cmp_k_w2, nsa_k_norm, True)
    vc = _compress(chunks(cv), cmp_v_pe, cmp_v_w1, cmp_v_b1, cmp_v_w2, nsa_k_norm, False)

    tdiag, band = _bias_tables(rel_bias)
    gates_t = misc[:, :NSA_HEADS * 3].reshape(b, s, g, NSA_GROUP * 3).transpose(0, 2, 3, 1)
    oa = _nsa(heads_first(qa, NSA_HEADS), kc, vc.transpose(0, 1, 3, 2), heads_first(ks, g), heads_t(sv, g),
              heads_first(kw, g), heads_t(wv, g), gates_t, band, tdiag, _overlap_matrix_t(s))

    iw_t = misc[:, NSA_HEADS * 3:NSA_HEADS * 3 + IDX_HEADS].reshape(b, s, IDX_HEADS).transpose(0, 2, 1)
    ob = _dsa(heads_first(qb, DSA_HEADS), heads_first(iq, IDX_HEADS), iw_t, ik.reshape(b, s, IDX_DIM),
              kb.reshape(b, s, HEAD_DIM), vb.reshape(b, s, HEAD_DIM).transpose(0, 2, 1), tdiag)

    x1, xn, moe_gates = _merge(x, oa, ob, ga, gb, w_branch_a, w_branch_b, w_out, ffn_norm, w_router,
                               b_router, tm=min(512, s))
    out = _moe(x1, xn, moe_gates, w_gate_up, b_gate_up, w_down, b_down, tm=min(512, n))
    return out.reshape(b, s, d)


def kernel(x, attn_norm, w_in, nsa_q_norm, nsa_k_norm, cmp_k_pe, cmp_k_w1, cmp_k_b1, cmp_k_w2, cmp_v_pe, cmp_v_w1, cmp_v_b1, cmp_v_w2, dsa_kv_norm, dsa_w_uk, dsa_w_uv, dsa_q_norm, dsa_k_norm, w_branch_a, w_branch_b, w_out, rel_bias, ffn_norm, w_router, b_router, w_gate_up, b_gate_up, w_down, b_down):
    for l in range(attn_norm.shape[0]):
        x = _layer(x, attn_norm[l], w_in[l], nsa_q_norm[l], nsa_k_norm[l], cmp_k_pe[l], cmp_k_w1[l],
                   cmp_k_b1[l], cmp_k_w2[l], cmp_v_pe[l], cmp_v_w1[l], cmp_v_b1[l], cmp_v_w2[l],
                   dsa_kv_norm[l], dsa_w_uk[l], dsa_w_uv[l], dsa_q_norm[l], dsa_k_norm[l],
                   w_branch_a[l], w_branch_b[l], w_out[l], rel_bias, ffn_norm[l], w_router[l],
                   b_router[l], w_gate_up[l], b_gate_up[l], w_down[l], b_down[l])
    return x
```

```python
import functools
import math

import numpy as np
import jax
import jax.numpy as jnp
from jax import lax
from jax.experimental import pallas as pl
from jax.experimental.pallas import tpu as pltpu

D_MODEL = 1024
HEAD_DIM = 64
NSA_HEADS = 8
NSA_KV_HEADS = 2
NSA_GROUP = NSA_HEADS // NSA_KV_HEADS
CMP_LEN = 32
CMP_STRIDE = 16
CMP_HID = 128
SLC_LEN = 64
SLC_TOPN = 16
WINDOW = 512
DSA_HEADS = 8
DSA_KV_RANK = 128
IDX_HEADS = 8
IDX_DIM = 64
DSA_TOPK_MAX = 256
N_EXPERTS = 32
MOE_TOPK = 4
D_EXPERT = 1024
SWIGLU_LIMIT = 7.0
SWIGLU_ALPHA = 1.702
REL_BUCKETS = 32
REL_MAX_DIST = 128
NORM_EPS = 1e-6
NEG = -1e30
LOG2E = math.log2(math.e)

Q_BLOCK = 128
KEY_CHUNK = 256
SCORE_UNROLL = 2
CMP_BAND = 24
COUNT_ROWS = 32
LANES = 128
SUBLANES = 8
MXU_DIM = 256
V_ROWS = HEAD_DIM + SUBLANES
MOE_TOKENS = 1024
MOE_ROWS = 160
VMEM_LIMIT = 56 * 1024 * 1024

F32 = jnp.float32
BF16 = jnp.bfloat16
I32 = jnp.int32


def _params(*sem):
    return pltpu.CompilerParams(dimension_semantics=sem, vmem_limit_bytes=VMEM_LIMIT)


def _dot(a, b):
    return jnp.dot(a, b, preferred_element_type=F32)


def _dot_nt(a, b):
    return lax.dot_general(a, b, (((1,), (1,)), ((), ())), preferred_element_type=F32)


def _const_spec(shape):
    nd = len(shape)
    return pl.BlockSpec(shape, lambda *_: (0,) * nd)


def _segment_sumsq(y, seg_ones):
    y2 = y * y
    hi = y2.astype(BF16)
    lo = (y2 - hi.astype(F32)).astype(BF16)
    return _dot(hi, seg_ones) + _dot(lo, seg_ones)


def _head_rms(y, seg_ones, gain):
    ms = _segment_sumsq(y, seg_ones) * (1.0 / HEAD_DIM)
    return y * lax.rsqrt(ms + NORM_EPS) * gain


def _row_rms(y, gain):
    ms = jnp.mean(y * y, axis=-1, keepdims=True)
    return y * lax.rsqrt(ms + NORM_EPS) * gain


def _in_proj_kernel(x_ref, g_ref, wqa_ref, wkv_ref, wmisc_ref, wqb_ref, wckv_ref, wiq_ref, wik_ref,
                    wga_ref, wgb_ref, seg512_ref, seg128_ref, gqa_ref, gka_ref, gqb_ref, gckv_ref,
                    wuk_ref, wuv_ref, gkb_ref,
                    qa_ref, ck_ref, cv_ref, ks_ref, sv_ref, kw_ref, wv_ref, misc_ref, qb_ref,
                    kb_ref, vb_ref, iq_ref, ik_ref, ga_ref, gb_ref):
    h = _row_rms(x_ref[...], g_ref[...]).astype(BF16)
    scale = HEAD_DIM ** -0.5 * LOG2E
    seg512 = seg512_ref[...]
    seg128 = seg128_ref[...]

    qa = _head_rms(_dot(h, wqa_ref[...]), seg512, gqa_ref[...])
    qa_ref[...] = (qa * scale).astype(BF16)

    kv = _dot(h, wkv_ref[...])
    ck_ref[...] = kv[:, 0:128]
    cv_ref[...] = kv[:, 128:256]
    ks_ref[...] = _head_rms(kv[:, 256:384], seg128, gka_ref[...]).astype(BF16)
    sv_ref[...] = kv[:, 384:512].astype(BF16)
    kw_ref[...] = _head_rms(kv[:, 512:640], seg128, gka_ref[...]).astype(BF16)
    wv_ref[...] = kv[:, 640:768].astype(BF16)

    misc = _dot(h, wmisc_ref[...])
    lane = lax.broadcasted_iota(I32, misc.shape, 1)
    iw_scale = (IDX_HEADS ** -0.5)
    idx_scale = IDX_DIM ** -0.5
    misc_ref[...] = jnp.where(lane < NSA_HEADS * 3, jax.nn.sigmoid(misc), (misc * iw_scale) * idx_scale)

    qb = _head_rms(_dot(h, wqb_ref[...]), seg512, gqb_ref[...])
    qb_ref[...] = (qb * scale).astype(BF16)

    ckv = _row_rms(_dot(h, wckv_ref[...]), gckv_ref[...]).astype(BF16)
    kb_ref[...] = _row_rms(_dot(ckv, wuk_ref[...]), gkb_ref[...]).astype(BF16)
    vb_ref[...] = _dot(ckv, wuv_ref[...]).astype(BF16)

    iq_ref[...] = _dot(h, wiq_ref[...]).astype(BF16)
    ik_ref[...] = _dot(h, wik_ref[...]).astype(BF16)
    ga_ref[...] = jax.nn.sigmoid(_dot(h, wga_ref[...]))
    gb_ref[...] = jax.nn.sigmoid(_dot(h, wgb_ref[...]))


def _in_proj(xf, attn_norm, w_in, nsa_q_norm, nsa_k_norm, dsa_q_norm, dsa_kv_norm, dsa_w_uk, dsa_w_uv,
             dsa_k_norm, tm):
    n = xf.shape[0]
    kvw = NSA_KV_HEADS * HEAD_DIM
    widths = (NSA_HEADS * HEAD_DIM, kvw, kvw, kvw, kvw, kvw, kvw, NSA_HEADS * 3,
              DSA_HEADS * HEAD_DIM, DSA_KV_RANK, IDX_HEADS * IDX_DIM, IDX_DIM, IDX_HEADS, D_MODEL, D_MODEL)
    offs = np.concatenate([[0], np.cumsum(widths)])
    sec = [w_in[:, offs[i]:offs[i + 1]] for i in range(len(widths))]
    wb = lambda w: w.astype(BF16)
    wqa = wb(sec[0])
    wkv = wb(jnp.concatenate(sec[1:7], axis=1))
    wmisc = wb(jnp.concatenate([sec[7], sec[12], jnp.zeros((D_MODEL, LANES - 32), F32)], axis=1))
    wqb, wckv, wiq, wik, wga, wgb = wb(sec[8]), wb(sec[9]), wb(sec[10]), wb(sec[11]), wb(sec[13]), wb(sec[14])

    def seg_ones(w):
        i = np.arange(w) // HEAD_DIM
        return jnp.asarray((i[:, None] == i[None, :]).astype(np.float32), BF16)

    consts = [attn_norm.reshape(1, -1), wqa, wkv, wmisc, wqb, wckv, wiq, wik, wga, wgb,
              seg_ones(512), seg_ones(128),
              jnp.tile(nsa_q_norm, NSA_HEADS).reshape(1, -1), jnp.tile(nsa_k_norm, NSA_KV_HEADS).reshape(1, -1),
              jnp.tile(dsa_q_norm, DSA_HEADS).reshape(1, -1), dsa_kv_norm.reshape(1, -1),
              wb(dsa_w_uk), wb(dsa_w_uv), dsa_k_norm.reshape(1, -1)]
    out_defs = [(512, BF16), (128, F32), (128, F32), (128, BF16), (128, BF16), (128, BF16), (128, BF16),
                (128, F32), (512, BF16), (64, BF16), (64, BF16), (512, BF16), (64, BF16),
                (D_MODEL, F32), (D_MODEL, F32)]
    row = lambda w: pl.BlockSpec((tm, w), lambda i: (i, 0))
    return pl.pallas_call(
        _in_proj_kernel,
        grid=(n // tm,),
        in_specs=[row(D_MODEL)] + [_const_spec(c.shape) for c in consts],
        out_specs=[row(w) for w, _ in out_defs],
        out_shape=[jax.ShapeDtypeStruct((n, w), dt) for w, dt in out_defs],
        compiler_params=_params("parallel"),
        name="in_proj",
    )(xf, *consts)


def _compress_kernel(c_ref, pea_ref, peb_ref, w1a_ref, w1b_ref, b1_ref, w2_ref, gain_ref, o_ref, *, norm, n_c):
    c = c_ref[0, 0]
    ua = _dot((c + pea_ref[...]).astype(BF16), w1a_ref[...])
    ub = _dot((c + peb_ref[...]).astype(BF16), w1b_ref[...])
    rows = c.shape[0]
    ub_next = pltpu.roll(ub, rows - 1, 0)
    hid = jax.nn.gelu(ua + ub_next + b1_ref[...])
    o = _dot(hid.astype(BF16), w2_ref[...])
    if norm:
        o = _row_rms(o, gain_ref[...])
    ridx = lax.broadcasted_iota(I32, o.shape, 0)
    o_ref[0, 0] = jnp.where(ridx < n_c, o, 0.0).astype(o_ref.dtype)


def _compress(c, pe, w1, b1, w2, gain, norm):
    b, g, rows, width = c.shape
    half = CMP_STRIDE * HEAD_DIM
    consts = [pe[:CMP_STRIDE].reshape(1, half), pe[CMP_STRIDE:].reshape(1, half),
              w1[:half].astype(BF16), w1[half:].astype(BF16), b1.reshape(1, -1), w2.astype(BF16),
              gain.reshape(1, -1)]
    return pl.pallas_call(
        functools.partial(_compress_kernel, norm=norm, n_c=rows - 1),
        grid=(b, g),
        in_specs=[pl.BlockSpec((1, 1, rows, width), lambda i, j: (i, j, 0, 0))] + [_const_spec(x.shape) for x in consts],
        out_specs=pl.BlockSpec((1, 1, rows, HEAD_DIM), lambda i, j: (i, j, 0, 0)),
        out_shape=jax.ShapeDtypeStruct((b, g, rows, HEAD_DIM), BF16),
        compiler_params=_params("parallel", "parallel"),
        name="compress_norm" if norm else "compress",
    )(c, *consts)


def _softmax_tile(slot, rows, ms, add, table, s_ref, p_ref):
    T = Q_BLOCK
    new_m, alphas = [], []
    for h in range(len(ms)):
        cols = slice(h * T, (h + 1) * T)
        s = s_ref[slot, :rows, cols]
        if add is not None:
            s = s + add
        if table is not None:
            s = s + table[:, cols]
        m_new = jnp.maximum(ms[h], jnp.max(s, axis=0, keepdims=True))
        alphas.append(jnp.exp2(ms[h] - m_new))
        new_m.append(m_new)
        p_ref[slot, :rows, cols] = jnp.exp2(s - m_new).astype(p_ref.dtype)
    return new_m, jnp.concatenate(alphas, axis=1)


def _attend_tiles(carry, q_ref, tiles, s_ref, p_ref):
    ms, acc = carry
    q_all = q_ref[0].reshape(len(ms) * Q_BLOCK, HEAD_DIM)
    rows = [t[0].shape[0] for t in tiles]
    s_ref[0, :rows[0]] = _dot_nt(tiles[0][0], q_all)
    alpha_prev = None
    for i, (_, _, add, table) in enumerate(tiles):
        slot = i % 2
        if i + 1 < len(tiles):
            s_ref[1 - slot, :rows[i + 1]] = _dot_nt(tiles[i + 1][0], q_all)
        ms, alpha = _softmax_tile(slot, rows[i], ms, add, table, s_ref, p_ref)
        if i > 0:
            acc = acc * alpha_prev + _dot(tiles[i - 1][1], p_ref[1 - slot, :rows[i - 1]])
        alpha_prev = alpha
    last = len(tiles) - 1
    return ms, acc * alpha_prev + _dot(tiles[last][1], p_ref[last % 2, :rows[last]])


def _attend_init(heads, rows=V_ROWS):
    return ([jnp.full((1, Q_BLOCK), NEG, F32) for _ in range(heads)],
            jnp.zeros((rows, heads * Q_BLOCK), F32))


def _attend_finish(carry):
    _, acc = carry
    return acc[:HEAD_DIM] / acc[HEAD_DIM:HEAD_DIM + 1]


def _far_attend(qi, heads, n_tiles, q_ref, load_k, load_vt, load_add, s_ref, p_ref):
    T, C = Q_BLOCK, KEY_CHUNK
    pairs = (qi // 2 + 1) // 2
    q_all = q_ref[0].reshape(heads * T, HEAD_DIM)
    tile = lambda j: pl.multiple_of(jnp.clip(j, 0, n_tiles - 1) * C, C)
    logits = lambda j: _dot_nt(load_k(tile(j)), q_all)
    s_ref[0] = logits(0)
    p_ref[1] = jnp.zeros(p_ref.shape[1:], p_ref.dtype)

    softmax = lambda slot, j, ms: _softmax_tile(slot, C, ms, load_add(tile(j)), None, s_ref, p_ref)

    def body(i, carry):
        ms, acc, alpha_prev = carry
        a = 2 * i
        s_ref[1] = logits(a + 1)
        ms, alpha_a = softmax(0, a, ms)
        acc = acc * alpha_prev + _dot(load_vt(tile(a - 1)), p_ref[1])
        s_ref[0] = logits(a + 2)
        ms, alpha_b = softmax(1, a + 1, ms)
        acc = acc * alpha_a + _dot(load_vt(tile(a)), p_ref[0])
        return ms, acc, alpha_b

    ms, acc = _attend_init(heads)
    ms, acc, alpha_prev = lax.fori_loop(0, pairs, body, (ms, acc, jnp.ones((1, heads * T), F32)))
    acc = acc * alpha_prev + _dot(load_vt(tile(2 * pairs - 1)), p_ref[1])
    return ms, acc


def _nsa_kernel(q_ref, kc_ref, vcx_ref, ks_ref, vst_ref, kw_ref, vwt_ref, gate_ref, band_ref, tdiag_ref,
                o_ref, sc_ref, sel_ref, s_ref, p_ref):
    qi = pl.program_id(2)
    hg, T, C = NSA_GROUP, Q_BLOCK, KEY_CHUNK
    t0 = qi * T
    n_cp = kc_ref.shape[2]
    n_s = sel_ref.shape[1]
    row_c = lax.broadcasted_iota(I32, (C, T), 0)
    tpos_c = t0 + lax.broadcasted_iota(I32, (C, T), 1)
    diag_start = pl.multiple_of(jnp.maximum(t0 - T, 0), T)

    band0 = pl.multiple_of(jnp.maximum(t0 // CMP_STRIDE - 2 * SUBLANES, 0), SUBLANES)
    sc_ref[...] = jnp.where(lax.broadcasted_iota(I32, sc_ref.shape, 0) >= band0 + CMP_BAND, NEG, 0.0)
    sc_ref[pl.ds(band0, CMP_BAND), :] = band_ref[0]
    rows_c = min(C, n_cp)
    cmp_tiles = [(kc_ref[0, 0, lo:lo + rows_c, :], vcx_ref[0, 0, :, lo:lo + rows_c], None,
                  sc_ref[lo:lo + rows_c, :]) for lo in range(0, n_cp, rows_c)]
    m_c, acc_c = _attend_tiles(_attend_init(hg, vcx_ref.shape[2]), q_ref, cmp_tiles, s_ref, p_ref)
    seen = jnp.concatenate(m_c, axis=1) > 0.5 * NEG
    acc_c = acc_c * jnp.where(seen, 1.0 / acc_c[HEAD_DIM:HEAD_DIM + 1], 0.0)
    o_c = [acc_c[:HEAD_DIM, h * T:(h + 1) * T] for h in range(hg)]
    imp = acc_c[V_ROWS:, 0:T]
    for h in range(1, hg):
        imp = imp + acc_c[V_ROWS:, h * T:(h + 1) * T]

    blk = lax.broadcasted_iota(I32, (n_s, T), 0)
    blk_f = blk.astype(F32)
    cur = (t0 + lax.broadcasted_iota(I32, (n_s, T), 1)) // SLC_LEN
    forced = (blk == 0) | (blk == cur) | (blk == cur - 1)
    work = jnp.where(forced, 1e6, jnp.where(blk <= cur, imp, -1e6))

    def pick(_, carry):
        work, sel = carry
        mx = jnp.max(work, axis=0, keepdims=True)
        first = jnp.min(jnp.where(work == mx, blk_f, float(n_s)), axis=0, keepdims=True)
        hit = blk_f == first
        return jnp.where(hit, -3e38, work), jnp.where(hit, 1.0, sel)

    _, sel = lax.fori_loop(0, min(SLC_TOPN, n_s), pick, (work, jnp.zeros((n_s, T), F32)))
    sel_add = jnp.where(sel > 0.5, 0.0, NEG)
    sel_ref[0] = sel_add
    sel_ref[1] = jnp.where(blk < (t0 - T) // SLC_LEN, sel_add, NEG)

    def block_mask(start, which):
        first = start // SLC_LEN
        return jnp.concatenate([jnp.broadcast_to(sel_ref[which, pl.ds(first + i, 1), :], (SLC_LEN, T))
                                for i in range(C // SLC_LEN)], axis=0)

    load_ks = lambda start: ks_ref[0, 0, pl.ds(start, C), :]
    load_vst = lambda start: vst_ref[0, 0, :, pl.ds(start, C)]
    carry = _far_attend(qi, hg, ks_ref.shape[2] // C, q_ref, load_ks, load_vst,
                        lambda start: block_mask(start, 1), s_ref, p_ref)
    last_tile = (load_ks(diag_start), load_vst(diag_start), block_mask(diag_start, 0), tdiag_ref[0])
    o_s = _attend_finish(_attend_tiles(carry, q_ref, [last_tile], s_ref, p_ref))

    def win_tile(lo, last):
        start = pl.multiple_of(jnp.maximum(lo, 0), T)
        pos = row_c + start
        rel = tpos_c - pos
        add = jnp.where((pos < lo + C) & (rel >= 0) & (rel < WINDOW), 0.0, NEG)
        return (kw_ref[0, 0, pl.ds(start, C), :], vwt_ref[0, 0, :, pl.ds(start, C)], add,
                tdiag_ref[0] if last else None)

    win_tiles = [win_tile(t0 - T - 2 * C, False), win_tile(t0 - T - C, False), win_tile(t0 - T, True)]
    o_w = _attend_finish(_attend_tiles(_attend_init(hg), q_ref, win_tiles, s_ref, p_ref))

    gates = gate_ref[0, 0]
    for h in range(hg):
        cols = slice(h * T, (h + 1) * T)
        o = (gates[3 * h:3 * h + 1] * o_c[h] + gates[3 * h + 1:3 * h + 2] * o_s[:, cols]
             + gates[3 * h + 2:3 * h + 3] * o_w[:, cols])
        o_ref[0, h] = o.T.astype(o_ref.dtype)


def _nsa(q, kc, vcx, ks, vst, kw, vwt, gates_t, band, tdiag):
    b, _, s, _ = q.shape
    g, hg = NSA_KV_HEADS, NSA_GROUP
    nq = s // Q_BLOCK
    n_cp = kc.shape[2]
    assert n_cp % min(KEY_CHUNK, n_cp) == 0
    rows = lambda r: pl.BlockSpec((1, 1, r, HEAD_DIM), lambda bi, gi, qi: (bi, gi, 0, 0))
    cols = lambda r, c: pl.BlockSpec((1, 1, r, c), lambda bi, gi, qi: (bi, gi, 0, 0))
    return pl.pallas_call(
        _nsa_kernel,
        grid=(b, g, nq),
        in_specs=[
            pl.BlockSpec((1, hg, Q_BLOCK, HEAD_DIM), lambda bi, gi, qi: (bi, gi, qi, 0)),
            rows(n_cp), cols(vcx.shape[2], n_cp), rows(s), cols(V_ROWS, s), rows(s), cols(V_ROWS, s),
            pl.BlockSpec((1, 1, hg * 3, Q_BLOCK), lambda bi, gi, qi: (bi, gi, 0, qi)),
            pl.BlockSpec((1, CMP_BAND, hg * Q_BLOCK), lambda bi, gi, qi: (jnp.minimum(qi, 2), 0, gi)),
            pl.BlockSpec((1, KEY_CHUNK, hg * Q_BLOCK), lambda bi, gi, qi: (jnp.minimum(qi, 1), 0, gi)),
        ],
        out_specs=pl.BlockSpec((1, hg, Q_BLOCK, HEAD_DIM), lambda bi, gi, qi: (bi, gi, qi, 0)),
        out_shape=jax.ShapeDtypeStruct((b, NSA_HEADS, s, HEAD_DIM), BF16),
        scratch_shapes=[pltpu.VMEM((n_cp, hg * Q_BLOCK), F32), pltpu.VMEM((2, s // SLC_LEN, Q_BLOCK), F32),
                        pltpu.VMEM((2, KEY_CHUNK, hg * Q_BLOCK), F32), pltpu.VMEM((2, KEY_CHUNK, hg * Q_BLOCK), BF16)],
        compiler_params=_params("parallel", "parallel", "arbitrary"),
        name="nsa",
    )(q, kc, vcx, ks, vst, kw, vwt, gates_t, band, tdiag)


def _sortable_key(x):
    b = lax.bitcast_convert_type(jnp.where(x == 0.0, 0.0, x), I32)
    return b ^ ((b >> 31) & 0x7FFFFFFF)


def _dsa_kernel(q_ref, iq_ref, iw_ref, ik_ref, k_ref, vt_ref, tdiag_ref, o_ref, key_ref, s_ref, p_ref,
                *, n_keep):
    qi = pl.program_id(1)
    H, T, C = DSA_HEADS, Q_BLOCK, KEY_CHUNK
    t0 = qi * T
    n_chunks = qi // 2 + 1
    iw = iw_ref[0]
    row = lax.broadcasted_iota(I32, (C, T), 0)
    tpos = t0 + lax.broadcasted_iota(I32, (C, T), 1)
    int_min = jnp.iinfo(jnp.int32).min

    def score_step(j, _):
        for u in range(SCORE_UNROLL):
            start = pl.multiple_of((j * SCORE_UNROLL + u) * C, C)
            ikc = ik_ref[0, pl.ds(start, C), :]
            score = jnp.maximum(_dot_nt(ikc, iq_ref[0, 0]), 0.0) * iw[0:1]
            for h in range(1, IDX_HEADS):
                score = score + jnp.maximum(_dot_nt(ikc, iq_ref[0, h]), 0.0) * iw[h:h + 1]
            score = jnp.where(row + start <= tpos, score, NEG)
            key_ref[pl.ds(start, C), :] = _sortable_key(score)
        return 0

    lax.fori_loop(0, (n_chunks + SCORE_UNROLL - 1) // SCORE_UNROLL, score_step, 0)

    def count(pred_fn):
        def body(j, acc):
            start = pl.multiple_of(j * C, C)
            hits = jnp.where(pred_fn(key_ref[pl.ds(start, C), :], start), 1.0, 0.0)
            return acc + jnp.sum(hits.reshape(C // COUNT_ROWS, COUNT_ROWS, T), axis=0)
        acc = lax.fori_loop(0, n_chunks, body, jnp.zeros((COUNT_ROWS, T), F32))
        return jnp.sum(acc, axis=0, keepdims=True).astype(I32)

    def thr_bit(i, carry):
        thr_u, n_at = carry
        cand_u = thr_u | lax.shift_left(jnp.int32(1), 31 - i)
        cand = cand_u ^ int_min
        n_ge = count(lambda keys, _: keys >= cand)
        ok = n_ge >= n_keep
        return jnp.where(ok, cand_u, thr_u), jnp.where(ok, n_ge, n_at)

    thr_u, n_at = lax.fori_loop(0, 32, thr_bit, (jnp.zeros((1, T), I32),
                                                 jnp.full((1, T), 1, I32) * (n_chunks * C)))
    thr = thr_u ^ int_min

    @pl.when(jnp.max((n_at - n_keep).astype(F32)) > 0.0)
    def _():
        need = n_keep - count(lambda keys, _: keys > thr)
        n_pos_bits = int(math.ceil(math.log2(key_ref.shape[0] + 1)))

        def pos_bit(i, lim):
            cand = lim + lax.shift_left(jnp.int32(1), n_pos_bits - 1 - i)
            n_tied = count(lambda keys, start: (keys == thr) & (row + start < cand))
            return jnp.where(n_tied <= need, cand, lim)

        lim = lax.fori_loop(0, n_pos_bits, pos_bit, jnp.zeros((1, T), I32))

        def drop(j, _):
            start = pl.multiple_of(j * C, C)
            keys = key_ref[pl.ds(start, C), :]
            key_ref[pl.ds(start, C), :] = jnp.where((keys == thr) & (row + start >= lim), int_min, keys)
            return 0

        lax.fori_loop(0, n_chunks, drop, 0)

    def key_mask(start, far):
        keep = key_ref[pl.ds(start, C), :] >= thr
        if far:
            keep = keep & (row + start < t0 - T)
        return jnp.where(keep, 0.0, NEG)

    load_k = lambda start: k_ref[0, pl.ds(start, C), :]
    load_vt = lambda start: vt_ref[0, :, pl.ds(start, C)]
    carry = _far_attend(qi, H, k_ref.shape[1] // C, q_ref, load_k, load_vt, lambda start: key_mask(start, True), s_ref, p_ref)
    diag_start = pl.multiple_of(jnp.maximum(t0 - T, 0), T)
    last_tile = (load_k(diag_start), load_vt(diag_start), key_mask(diag_start, False), tdiag_ref[0])
    o = _attend_finish(_attend_tiles(carry, q_ref, [last_tile], s_ref, p_ref))
    for h in range(H):
        o_ref[0, h] = o[:, h * T:(h + 1) * T].T.astype(o_ref.dtype)


def _dsa(q, iq, iw_t, ik, k, vt, tdiag):
    b, _, s, _ = q.shape
    assert s % (SCORE_UNROLL * KEY_CHUNK) == 0
    nq = s // Q_BLOCK
    n_keep = min(DSA_TOPK_MAX, s // 4)
    heads = lambda: pl.BlockSpec((1, DSA_HEADS, Q_BLOCK, HEAD_DIM), lambda bi, qi: (bi, 0, qi, 0))
    rows = lambda: pl.BlockSpec((1, s, HEAD_DIM), lambda bi, qi: (bi, 0, 0))
    return pl.pallas_call(
        functools.partial(_dsa_kernel, n_keep=n_keep),
        grid=(b, nq),
        in_specs=[heads(), heads(),
                  pl.BlockSpec((1, IDX_HEADS, Q_BLOCK), lambda bi, qi: (bi, 0, qi)),
                  rows(), rows(),
                  pl.BlockSpec((1, V_ROWS, s), lambda bi, qi: (bi, 0, 0)),
                  pl.BlockSpec((1, KEY_CHUNK, DSA_HEADS * Q_BLOCK), lambda bi, qi: (jnp.minimum(qi, 1), 0, 1))],
        out_specs=heads(),
        out_shape=jax.ShapeDtypeStruct((b, DSA_HEADS, s, HEAD_DIM), BF16),
        scratch_shapes=[pltpu.VMEM((s, Q_BLOCK), I32), pltpu.VMEM((2, KEY_CHUNK, DSA_HEADS * Q_BLOCK), F32),
                        pltpu.VMEM((2, KEY_CHUNK, DSA_HEADS * Q_BLOCK), BF16)],
        compiler_params=_params("parallel", "arbitrary"),
        name="dsa",
    )(q, iq, iw_t, ik, k, vt, tdiag)


def _merge_kernel(x_ref, oa_ref, ob_ref, ga_ref, gb_ref, wa_ref, wb_ref, wo_ref, gn_ref, wr_ref, br_ref,
                  x1_ref, xn_ref, gate_ref, gate_t_ref):
    def branch(o_ref, w_ref):
        acc = _dot(o_ref[0, 0], w_ref[0])
        for h in range(1, o_ref.shape[1]):
            acc = acc + _dot(o_ref[0, h], w_ref[h])
        return acc

    merged = ga_ref[...] * branch(oa_ref, wa_ref) + gb_ref[...] * branch(ob_ref, wb_ref)
    x1 = x_ref[...] + _dot(merged.astype(BF16), wo_ref[...])
    x1_ref[...] = x1
    xn = _row_rms(x1, gn_ref[...]).astype(BF16)
    xn_ref[...] = xn

    logits = _dot(xn, wr_ref[...]) + br_ref[...]
    lane = lax.broadcasted_iota(I32, logits.shape, 1)
    work = jnp.where(lane < N_EXPERTS, logits, -3e38)
    gates = jnp.zeros_like(logits)
    denom = jnp.zeros((logits.shape[0], 1), F32)
    top = None
    for _ in range(MOE_TOPK):
        mx = jnp.max(work, axis=1, keepdims=True)
        first = jnp.min(jnp.where(work == mx, lane, LANES), axis=1, keepdims=True)
        hit = lane == first
        top = mx if top is None else top
        e = jnp.exp(mx - top)
        gates = jnp.where(hit, e, gates)
        denom = denom + e
        work = jnp.where(hit, -3e38, work)
    gates = gates / denom
    gate_ref[...] = gates
    gate_t_ref[...] = gates.T


def _merge(x, oa, ob, ga, gb, w_branch_a, w_branch_b, w_out, ffn_norm, w_router, b_router, tm):
    b, s, d = x.shape
    n = b * s
    xf = x.reshape(n, d)
    nt = s // tm
    wa = w_branch_a.reshape(NSA_HEADS, HEAD_DIM, d).astype(BF16)
    wb = w_branch_b.reshape(DSA_HEADS, HEAD_DIM, d).astype(BF16)
    wr = jnp.concatenate([w_router, jnp.zeros((d, LANES - N_EXPERTS), F32)], axis=1).astype(BF16)
    br = jnp.concatenate([b_router, jnp.zeros((LANES - N_EXPERTS,), F32)]).reshape(1, LANES)
    consts = [wa, wb, w_out.astype(BF16), ffn_norm.reshape(1, d), wr, br]
    row = lambda w: pl.BlockSpec((tm, w), lambda bi, ti: (bi * nt + ti, 0))
    heads = lambda: pl.BlockSpec((1, NSA_HEADS, tm, HEAD_DIM), lambda bi, ti: (bi, 0, ti, 0))
    return pl.pallas_call(
        _merge_kernel,
        grid=(b, nt),
        in_specs=[row(d), heads(), heads(), row(d), row(d)] + [_const_spec(c.shape) for c in consts],
        out_specs=[row(d), row(d), row(LANES), pl.BlockSpec((LANES, tm), lambda bi, ti: (0, bi * nt + ti))],
        out_shape=[jax.ShapeDtypeStruct((n, d), F32), jax.ShapeDtypeStruct((n, d), BF16),
                   jax.ShapeDtypeStruct((n, LANES), F32), jax.ShapeDtypeStruct((LANES, n), F32)],
        compiler_params=_params("parallel", "parallel"),
        name="merge",
    )(xf, oa, ob, ga, gb, *consts)


def _deinterleave_index(width):
    half = MXU_DIM // 2
    c = np.arange(width)
    r = c % MXU_DIM
    return (c - r) + np.where(r < half, 2 * r, 2 * (r - half) + 1)


def _moe_prep_kernel(w_ref, perm_ref, o_ref):
    perm = perm_ref[...]
    for g in range(w_ref.shape[2] // MXU_DIM):
        cols = slice(g * MXU_DIM, (g + 1) * MXU_DIM)
        o_ref[0, :, cols] = _dot(w_ref[0, :, cols].astype(BF16), perm).astype(BF16)


def _moe_prep(w_gate_up, tr):
    e, d, f2 = w_gate_up.shape
    src = _deinterleave_index(MXU_DIM)
    perm = jnp.asarray((np.arange(MXU_DIM)[:, None] == src[None, :]).astype(np.float32), BF16)
    return pl.pallas_call(
        _moe_prep_kernel,
        grid=(e, d // tr),
        in_specs=[pl.BlockSpec((1, tr, f2), lambda i, j: (i, j, 0)), _const_spec(perm.shape)],
        out_specs=pl.BlockSpec((1, tr, f2), lambda i, j: (i, j, 0)),
        out_shape=jax.ShapeDtypeStruct((e, d, f2), BF16),
        compiler_params=_params("parallel", "parallel"),
        name="moe_prep",
    )(w_gate_up, perm)


def _moe_kernel(x1_ref, xn_ref, gate_ref, gate_t_ref, lower_ref, upper_ref, wgu_ref, bgu_ref, wd_ref, bd_ref,
                o_ref, rank_ref, rank_t_ref):
    e = pl.program_id(1)
    tm, R = xn_ref.shape[0], MOE_ROWS

    @pl.when(e == 0)
    def _():
        o_ref[...] = x1_ref[...]
        routed = jnp.where(gate_ref[...] > 0.0, 1.0, 0.0).astype(BF16)
        routed_t = jnp.where(gate_t_ref[...] > 0.0, 1.0, 0.0).astype(BF16)
        rank_ref[...] = _dot(lower_ref[...], routed)
        rank_t_ref[...] = _dot(routed_t, upper_ref[...])

    g_row = gate_t_ref[pl.ds(e, 1), :]
    is_e = lax.broadcasted_iota(I32, (tm, LANES), 1) == e
    g_col = jnp.sum(jnp.where(is_e, gate_ref[...], 0.0), axis=1, keepdims=True)
    rank_row = jnp.where(g_row > 0.0, rank_t_ref[pl.ds(e, 1), :], -1.0)
    rank_col = jnp.where(g_col > 0.0, jnp.sum(jnp.where(is_e, rank_ref[...], 0.0), axis=1, keepdims=True), -1.0)
    n_routed = jnp.sum(jnp.where(g_row > 0.0, 1.0, 0.0)).astype(I32)
    slot_r = lax.broadcasted_iota(I32, (R, tm), 0).astype(F32)
    slot_c = lax.broadcasted_iota(I32, (tm, R), 1).astype(F32)
    half = MXU_DIM // 2

    def block(j, _):
        base = (j * R).astype(F32)
        pick = jnp.where(rank_row - base == slot_r, 1.0, 0.0)
        xg = _dot(pick.astype(BF16), xn_ref[...]).astype(BF16)
        hu = _dot(xg, wgu_ref[0]) + bgu_ref[0]
        acts = []
        for g in range(hu.shape[1] // MXU_DIM):
            glu = jnp.minimum(hu[:, g * MXU_DIM:g * MXU_DIM + half], SWIGLU_LIMIT)
            lin = jnp.clip(hu[:, g * MXU_DIM + half:(g + 1) * MXU_DIM], -SWIGLU_LIMIT, SWIGLU_LIMIT)
            acts.append((glu * jax.nn.sigmoid(SWIGLU_ALPHA * glu) * (lin + 1.0)).astype(BF16))
        y = _dot(jnp.concatenate(acts, axis=1), wd_ref[0]) + bd_ref[0]
        g_r = jnp.sum(pick * g_row, axis=1, keepdims=True)
        put = jnp.where(rank_col - base == slot_c, 1.0, 0.0).astype(BF16)
        o_ref[...] += _dot(put, (y * g_r).astype(BF16))
        return 0

    lax.fori_loop(0, (n_routed + R - 1) // R, block, 0)


def _moe(x1, xn, gates, gates_t, w_gate_up, b_gate_up, w_down, b_down, tm):
    n, d = x1.shape
    f = D_EXPERT
    wgu = _moe_prep(w_gate_up, tr=256)
    bgu = b_gate_up[:, _deinterleave_index(2 * f)].reshape(N_EXPERTS, 1, 2 * f)
    wd = w_down.astype(BF16)
    bd = b_down.reshape(N_EXPERTS, 1, d)
    earlier = np.arange(tm)[None, :] < np.arange(tm)[:, None]
    lower = jnp.asarray(earlier.astype(np.float32), BF16)
    upper = jnp.asarray(earlier.T.astype(np.float32), BF16)
    row = lambda w: pl.BlockSpec((tm, w), lambda i, e: (i, 0))
    per_e = lambda a, b_: pl.BlockSpec((1, a, b_), lambda i, e: (e, 0, 0))
    return pl.pallas_call(
        _moe_kernel,
        grid=(n // tm, N_EXPERTS),
        in_specs=[row(d), row(d), row(LANES), pl.BlockSpec((LANES, tm), lambda i, e: (0, i)),
                  _const_spec(lower.shape), _const_spec(upper.shape),
                  per_e(d, 2 * f), per_e(1, 2 * f), per_e(f, d), per_e(1, d)],
        out_specs=row(d),
        out_shape=jax.ShapeDtypeStruct((n, d), F32),
        scratch_shapes=[pltpu.VMEM((tm, LANES), F32), pltpu.VMEM((LANES, tm), F32)],
        compiler_params=_params("parallel", "arbitrary"),
        name="moe",
    )(x1, xn, gates, gates_t, lower, upper, wgu, bgu, wd, bd)


def _rel_bucket(rel):
    n = jnp.maximum(rel, 0)
    exact = REL_BUCKETS // 2
    nf = jnp.maximum(n, 1).astype(F32)
    large = exact + (jnp.log(nf / exact) / math.log(REL_MAX_DIST / exact) * (REL_BUCKETS - exact)).astype(I32)
    large = jnp.minimum(large, REL_BUCKETS - 1)
    return jnp.where(n < exact, n, large)


def _rel_table(rel_bias, rel):
    far = rel_bias[REL_BUCKETS - 1]
    tab = jnp.where(rel[..., None] >= 0, (rel_bias[_rel_bucket(rel)] - far) * LOG2E, NEG)
    tab = jnp.moveaxis(tab, -1, -2)
    return tab.reshape(tab.shape[:-2] + (-1,)).astype(F32)


def _bias_tables(rel_bias):
    i = np.arange(Q_BLOCK)[None, :]
    j = np.arange(KEY_CHUNK)[:, None]
    tdiag = _rel_table(rel_bias, jnp.asarray(np.stack([i - j, i - j + Q_BLOCK])))
    r = np.arange(CMP_BAND)[:, None]
    rel_band = np.stack([i - (CMP_LEN - 1) - CMP_STRIDE * (c0 + r) for c0 in (0, -SUBLANES, -2 * SUBLANES)])
    band = _rel_table(rel_bias[:, :NSA_HEADS], jnp.asarray(rel_band))
    return tdiag, band


def _overlap_matrix_t(s):
    n_cp = s // CMP_STRIDE
    n_s = s // SLC_LEN
    c_start = np.arange(n_cp) * CMP_STRIDE
    s_start = np.arange(n_s) * SLC_LEN
    ovl = (c_start[None, :] < s_start[:, None] + SLC_LEN) & (c_start[None, :] + CMP_LEN > s_start[:, None])
    ovl[:, n_cp - 1] = False
    return jnp.asarray(ovl.astype(np.float32), BF16)


def _layer(x, attn_norm, w_in, nsa_q_norm, nsa_k_norm, cmp_k_pe, cmp_k_w1, cmp_k_b1, cmp_k_w2,
           cmp_v_pe, cmp_v_w1, cmp_v_b1, cmp_v_w2, dsa_kv_norm, dsa_w_uk, dsa_w_uv, dsa_q_norm,
           dsa_k_norm, w_branch_a, w_branch_b, w_out, rel_bias, ffn_norm, w_router, b_router,
           w_gate_up, b_gate_up, w_down, b_down):
    b, s, d = x.shape
    n = b * s
    g = NSA_KV_HEADS
    (qa, ck, cv, ks, sv, kw, wv, misc, qb, kb, vb, iq, ik, ga, gb) = _in_proj(
        x.reshape(n, d), attn_norm, w_in, nsa_q_norm, nsa_k_norm, dsa_q_norm, dsa_kv_norm, dsa_w_uk,
        dsa_w_uv, dsa_k_norm, tm=min(256, s))

    heads_first = lambda a, h: a.reshape(b, s, h, HEAD_DIM).transpose(0, 2, 1, 3)
    heads_t = lambda a, h: a.reshape(b, s, h, HEAD_DIM).transpose(0, 2, 3, 1)
    ones_rows = lambda a: jnp.concatenate(
        [a, jnp.ones(a.shape[:-2] + (V_ROWS - HEAD_DIM, a.shape[-1]), a.dtype)], axis=-2)
    chunks = lambda a: heads_first(a, g).reshape(b, g, s // CMP_STRIDE, CMP_STRIDE * HEAD_DIM)
    kc = _compress(chunks(ck), cmp_k_pe, cmp_k_w1, cmp_k_b1, cmp_k_w2, nsa_k_norm, True)
    vc = _compress(chunks(cv), cmp_v_pe, cmp_v_w1, cmp_v_b1, cmp_v_w2, nsa_k_norm, False)

    tdiag, band = _bias_tables(rel_bias)
    gates_t = misc[:, :NSA_HEADS * 3].reshape(b, s, g, NSA_GROUP * 3).transpose(0, 2, 3, 1)
    overlap_t = _overlap_matrix_t(s)
    vcx = jnp.concatenate([ones_rows(vc.transpose(0, 1, 3, 2)),
                           jnp.broadcast_to(overlap_t, (b, g) + overlap_t.shape)], axis=2)
    oa = _nsa(heads_first(qa, NSA_HEADS), kc, vcx, heads_first(ks, g), ones_rows(heads_t(sv, g)),
              heads_first(kw, g), ones_rows(heads_t(wv, g)), gates_t, band, tdiag)

    iw_t = misc[:, NSA_HEADS * 3:NSA_HEADS * 3 + IDX_HEADS].reshape(b, s, IDX_HEADS).transpose(0, 2, 1)
    ob = _dsa(heads_first(qb, DSA_HEADS), heads_first(iq, IDX_HEADS), iw_t, ik.reshape(b, s, IDX_DIM),
              kb.reshape(b, s, HEAD_DIM), ones_rows(vb.reshape(b, s, HEAD_DIM).transpose(0, 2, 1)), tdiag)

    x1, xn, moe_gates, moe_gates_t = _merge(x, oa, ob, ga, gb, w_branch_a, w_branch_b, w_out, ffn_norm, w_router,
                               b_router, tm=min(512, s))
    out = _moe(x1, xn, moe_gates, moe_gates_t, w_gate_up, b_gate_up, w_down, b_down, tm=min(MOE_TOKENS, n))
    return out.reshape(b, s, d)


def kernel(x, attn_norm, w_in, nsa_q_norm, nsa_k_norm, cmp_k_pe, cmp_k_w1, cmp_k_b1, cmp_k_w2, cmp_v_pe, cmp_v_w1, cmp_v_b1, cmp_v_w2, dsa_kv_norm, dsa_w_uk, dsa_w_uv, dsa_q_norm, dsa_k_norm, w_branch_a, w_branch_b, w_out, rel_bias, ffn_norm, w_router, b_router, w_gate_up, b_gate_up, w_down, b_down):
    for l in range(attn_norm.shape[0]):
        x = _layer(x, attn_norm[l], w_in[l], nsa_q_norm[l], nsa_k_norm[l], cmp_k_pe[l], cmp_k_w1[l],
                   cmp_k_b1[l], cmp_k_w2[l], cmp_v_pe[l], cmp_v_w1[l], cmp_v_b1[l], cmp_v_w2[l],
                   dsa_kv_norm[l], dsa_w_uk[l], dsa_w_uv[l], dsa_q_norm[l], dsa_k_norm[l],
                   w_branch_a[l], w_branch_b[l], w_out[l], rel_bias, ffn_norm[l], w_router[l],
                   b_router[l], w_gate_up[l], b_gate_up[l], w_down[l], b_down[l])
    return x
```

```python
import functools
import math

import numpy as np
import jax
import jax.numpy as jnp
from jax import lax
from jax.experimental import pallas as pl
from jax.experimental.pallas import tpu as pltpu

D_MODEL = 1024
HEAD_DIM = 64
NSA_HEADS = 8
NSA_KV_HEADS = 2
NSA_GROUP = NSA_HEADS // NSA_KV_HEADS
CMP_LEN = 32
CMP_STRIDE = 16
CMP_HID = 128
SLC_LEN = 64
SLC_TOPN = 16
WINDOW = 512
DSA_HEADS = 8
DSA_KV_RANK = 128
IDX_HEADS = 8
IDX_DIM = 64
DSA_TOPK_MAX = 256
N_EXPERTS = 32
MOE_TOPK = 4
D_EXPERT = 1024
SWIGLU_LIMIT = 7.0
SWIGLU_ALPHA = 1.702
REL_BUCKETS = 32
REL_MAX_DIST = 128
NORM_EPS = 1e-6
NEG = -1e30
LOG2E = math.log2(math.e)

Q_BLOCK = 128
KEY_CHUNK = 256
CMP_BAND = 24
COUNT_ROWS = 32
LANES = 128
SUBLANES = 8
MXU_DIM = 256
V_ROWS = HEAD_DIM + SUBLANES
MOE_TOKENS = 1024
MOE_ROWS = 160
VMEM_LIMIT = 56 * 1024 * 1024

F32 = jnp.float32
BF16 = jnp.bfloat16
I32 = jnp.int32


def _params(*sem):
    return pltpu.CompilerParams(dimension_semantics=sem, vmem_limit_bytes=VMEM_LIMIT)


def _dot(a, b):
    return jnp.dot(a, b, preferred_element_type=F32)


def _dot_nt(a, b):
    return lax.dot_general(a, b, (((1,), (1,)), ((), ())), preferred_element_type=F32)


def _const_spec(shape):
    nd = len(shape)
    return pl.BlockSpec(shape, lambda *_: (0,) * nd)


def _segment_sumsq(y, seg_ones):
    y2 = y * y
    hi = y2.astype(BF16)
    lo = (y2 - hi.astype(F32)).astype(BF16)
    return _dot(hi, seg_ones) + _dot(lo, seg_ones)


def _head_rms(y, seg_ones, gain):
    ms = _segment_sumsq(y, seg_ones) * (1.0 / HEAD_DIM)
    return y * lax.rsqrt(ms + NORM_EPS) * gain


def _row_rms(y, gain):
    ms = jnp.mean(y * y, axis=-1, keepdims=True)
    return y * lax.rsqrt(ms + NORM_EPS) * gain


def _in_proj_kernel(x_ref, g_ref, wqa_ref, wkv_ref, wmisc_ref, wqb_ref, wckv_ref, wiq_ref, wik_ref,
                    wga_ref, wgb_ref, seg512_ref, seg128_ref, gqa_ref, gka_ref, gqb_ref, gckv_ref,
                    wuk_ref, wuv_ref, gkb_ref,
                    qa_ref, ck_ref, cv_ref, ks_ref, sv_ref, kw_ref, wv_ref, misc_ref, qb_ref,
                    kb_ref, vb_ref, iq_ref, ik_ref, ga_ref, gb_ref):
    h = _row_rms(x_ref[...], g_ref[...]).astype(BF16)
    scale = HEAD_DIM ** -0.5 * LOG2E
    seg512 = seg512_ref[...]
    seg128 = seg128_ref[...]

    def put_heads(o_ref, y):
        for i in range(o_ref.shape[1]):
            o_ref[0, i] = y[:, i * HEAD_DIM:(i + 1) * HEAD_DIM].astype(o_ref.dtype)

    def put_values_t(o_ref, y):
        y_t = y.T
        ones = jnp.ones((V_ROWS - HEAD_DIM, y.shape[0]), o_ref.dtype)
        for i in range(o_ref.shape[1]):
            o_ref[0, i, :HEAD_DIM, :] = y_t[i * HEAD_DIM:(i + 1) * HEAD_DIM].astype(o_ref.dtype)
            o_ref[0, i, HEAD_DIM:, :] = ones

    qa = _head_rms(_dot(h, wqa_ref[...]), seg512, gqa_ref[...])
    put_heads(qa_ref, qa * scale)

    kv = _dot(h, wkv_ref[...])
    ck_ref[...] = kv[:, 0:128]
    cv_ref[...] = kv[:, 128:256]
    put_heads(ks_ref, _head_rms(kv[:, 256:384], seg128, gka_ref[...]))
    put_values_t(sv_ref, kv[:, 384:512])
    put_heads(kw_ref, _head_rms(kv[:, 512:640], seg128, gka_ref[...]))
    put_values_t(wv_ref, kv[:, 640:768])

    misc = _dot(h, wmisc_ref[...])
    lane = lax.broadcasted_iota(I32, misc.shape, 1)
    iw_scale = (IDX_HEADS ** -0.5)
    idx_scale = IDX_DIM ** -0.5
    misc = jnp.where(lane < NSA_HEADS * 3, jax.nn.sigmoid(misc), (misc * iw_scale) * idx_scale)
    misc_ref[0] = misc.T[:misc_ref.shape[1]]

    qb = _head_rms(_dot(h, wqb_ref[...]), seg512, gqb_ref[...])
    put_heads(qb_ref, qb * scale)

    ckv = _row_rms(_dot(h, wckv_ref[...]), gckv_ref[...]).astype(BF16)
    kb_ref[...] = _row_rms(_dot(ckv, wuk_ref[...]), gkb_ref[...]).astype(BF16)
    vb = _dot(ckv, wuv_ref[...])
    vb_ref[0, :HEAD_DIM, :] = vb.T.astype(BF16)
    vb_ref[0, HEAD_DIM:, :] = jnp.ones((V_ROWS - HEAD_DIM, vb.shape[0]), BF16)

    put_heads(iq_ref, _dot(h, wiq_ref[...]))
    ik_ref[...] = _dot(h, wik_ref[...]).astype(BF16)
    ga_ref[...] = jax.nn.sigmoid(_dot(h, wga_ref[...]))
    gb_ref[...] = jax.nn.sigmoid(_dot(h, wgb_ref[...]))


def _in_proj(x, attn_norm, w_in, nsa_q_norm, nsa_k_norm, dsa_q_norm, dsa_kv_norm, dsa_w_uk, dsa_w_uv,
             dsa_k_norm, tm):
    b, s, _ = x.shape
    n = b * s
    xf = x.reshape(n, -1)
    kvw = NSA_KV_HEADS * HEAD_DIM
    widths = (NSA_HEADS * HEAD_DIM, kvw, kvw, kvw, kvw, kvw, kvw, NSA_HEADS * 3,
              DSA_HEADS * HEAD_DIM, DSA_KV_RANK, IDX_HEADS * IDX_DIM, IDX_DIM, IDX_HEADS, D_MODEL, D_MODEL)
    offs = np.concatenate([[0], np.cumsum(widths)])
    sec = [w_in[:, offs[i]:offs[i + 1]] for i in range(len(widths))]
    wb = lambda w: w.astype(BF16)
    wqa = wb(sec[0])
    wkv = wb(jnp.concatenate(sec[1:7], axis=1))
    wmisc = wb(jnp.concatenate([sec[7], sec[12], jnp.zeros((D_MODEL, LANES - 32), F32)], axis=1))
    wqb, wckv, wiq, wik, wga, wgb = wb(sec[8]), wb(sec[9]), wb(sec[10]), wb(sec[11]), wb(sec[13]), wb(sec[14])

    def seg_ones(w):
        i = np.arange(w) // HEAD_DIM
        return jnp.asarray((i[:, None] == i[None, :]).astype(np.float32), BF16)

    consts = [attn_norm.reshape(1, -1), wqa, wkv, wmisc, wqb, wckv, wiq, wik, wga, wgb,
              seg_ones(512), seg_ones(128),
              jnp.tile(nsa_q_norm, NSA_HEADS).reshape(1, -1), jnp.tile(nsa_k_norm, NSA_KV_HEADS).reshape(1, -1),
              jnp.tile(dsa_q_norm, DSA_HEADS).reshape(1, -1), dsa_kv_norm.reshape(1, -1),
              wb(dsa_w_uk), wb(dsa_w_uv), dsa_k_norm.reshape(1, -1)]
    nt = s // tm
    g = NSA_KV_HEADS
    rows = lambda w, dt: (jax.ShapeDtypeStruct((n, w), dt), pl.BlockSpec((tm, w), lambda bi, ti: (bi * nt + ti, 0)))
    heads = lambda hh: (jax.ShapeDtypeStruct((b, hh, s, HEAD_DIM), BF16),
                        pl.BlockSpec((1, hh, tm, HEAD_DIM), lambda bi, ti: (bi, 0, ti, 0)))
    values_t = (jax.ShapeDtypeStruct((b, g, V_ROWS, s), BF16),
                pl.BlockSpec((1, g, V_ROWS, tm), lambda bi, ti: (bi, 0, 0, ti)))
    cols = lambda r, dt: (jax.ShapeDtypeStruct((b, r, s), dt), pl.BlockSpec((1, r, tm), lambda bi, ti: (bi, 0, ti)))
    outs = [heads(NSA_HEADS), rows(128, F32), rows(128, F32), heads(g), values_t, heads(g), values_t,
            cols(32, F32), heads(DSA_HEADS), rows(HEAD_DIM, BF16), cols(V_ROWS, BF16), heads(IDX_HEADS),
            rows(IDX_DIM, BF16), rows(D_MODEL, F32), rows(D_MODEL, F32)]
    return pl.pallas_call(
        _in_proj_kernel,
        grid=(b, nt),
        in_specs=[rows(D_MODEL, F32)[1]] + [_const_spec(c.shape) for c in consts],
        out_specs=[spec for _, spec in outs],
        out_shape=[shape for shape, _ in outs],
        compiler_params=_params("parallel", "parallel"),
        name="in_proj",
    )(xf, *consts)


def _compress_kernel(c_ref, pea_ref, peb_ref, w1a_ref, w1b_ref, b1_ref, w2_ref, gain_ref, o_ref, *, norm, n_c):
    c = c_ref[0, 0]
    ua = _dot((c + pea_ref[...]).astype(BF16), w1a_ref[...])
    ub = _dot((c + peb_ref[...]).astype(BF16), w1b_ref[...])
    rows = c.shape[0]
    ub_next = pltpu.roll(ub, rows - 1, 0)
    hid = jax.nn.gelu(ua + ub_next + b1_ref[...])
    o = _dot(hid.astype(BF16), w2_ref[...])
    if norm:
        o = _row_rms(o, gain_ref[...])
    ridx = lax.broadcasted_iota(I32, o.shape, 0)
    o_ref[0, 0] = jnp.where(ridx < n_c, o, 0.0).astype(o_ref.dtype)


def _compress(c, pe, w1, b1, w2, gain, norm):
    b, g, rows, width = c.shape
    half = CMP_STRIDE * HEAD_DIM
    consts = [pe[:CMP_STRIDE].reshape(1, half), pe[CMP_STRIDE:].reshape(1, half),
              w1[:half].astype(BF16), w1[half:].astype(BF16), b1.reshape(1, -1), w2.astype(BF16),
              gain.reshape(1, -1)]
    return pl.pallas_call(
        functools.partial(_compress_kernel, norm=norm, n_c=rows - 1),
        grid=(b, g),
        in_specs=[pl.BlockSpec((1, 1, rows, width), lambda i, j: (i, j, 0, 0))] + [_const_spec(x.shape) for x in consts],
        out_specs=pl.BlockSpec((1, 1, rows, HEAD_DIM), lambda i, j: (i, j, 0, 0)),
        out_shape=jax.ShapeDtypeStruct((b, g, rows, HEAD_DIM), BF16),
        compiler_params=_params("parallel", "parallel"),
        name="compress_norm" if norm else "compress",
    )(c, *consts)


def _softmax_tile(slot, rows, ms, add, table, s_ref, p_ref):
    T = Q_BLOCK
    new_m, alphas = [], []
    for h in range(len(ms)):
        cols = slice(h * T, (h + 1) * T)
        s = s_ref[slot, :rows, cols]
        if add is not None:
            s = s + add
        if table is not None:
            s = s + table[:, cols]
        m_new = jnp.maximum(ms[h], jnp.max(s, axis=0, keepdims=True))
        alphas.append(jnp.exp2(ms[h] - m_new))
        new_m.append(m_new)
        p_ref[slot, :rows, cols] = jnp.exp2(s - m_new).astype(p_ref.dtype)
    return new_m, jnp.concatenate(alphas, axis=1)


def _attend_tiles(carry, q_ref, tiles, s_ref, p_ref):
    ms, acc = carry
    q_all = q_ref[0].reshape(len(ms) * Q_BLOCK, HEAD_DIM)
    rows = [t[0].shape[0] for t in tiles]
    s_ref[0, :rows[0]] = _dot_nt(tiles[0][0], q_all)
    alpha_prev = None
    for i, (_, _, add, table) in enumerate(tiles):
        slot = i % 2
        if i + 1 < len(tiles):
            s_ref[1 - slot, :rows[i + 1]] = _dot_nt(tiles[i + 1][0], q_all)
        ms, alpha = _softmax_tile(slot, rows[i], ms, add, table, s_ref, p_ref)
        if i > 0:
            acc = acc * alpha_prev + _dot(tiles[i - 1][1], p_ref[1 - slot, :rows[i - 1]])
        alpha_prev = alpha
    last = len(tiles) - 1
    return ms, acc * alpha_prev + _dot(tiles[last][1], p_ref[last % 2, :rows[last]])


def _attend_init(heads, rows=V_ROWS):
    return ([jnp.full((1, Q_BLOCK), NEG, F32) for _ in range(heads)],
            jnp.zeros((rows, heads * Q_BLOCK), F32))


def _attend_finish(carry):
    _, acc = carry
    return acc[:HEAD_DIM] / acc[HEAD_DIM:HEAD_DIM + 1]


def _far_attend(qi, heads, n_tiles, q_ref, load_k, load_vt, load_add, s_ref, p_ref):
    T, C = Q_BLOCK, KEY_CHUNK
    pairs = (qi // 2 + 1) // 2
    q_all = q_ref[0].reshape(heads * T, HEAD_DIM)
    tile = lambda j: pl.multiple_of(jnp.clip(j, 0, n_tiles - 1) * C, C)
    logits = lambda j: _dot_nt(load_k(tile(j)), q_all)
    s_ref[0] = logits(0)
    p_ref[1] = jnp.zeros(p_ref.shape[1:], p_ref.dtype)

    softmax = lambda slot, j, ms: _softmax_tile(slot, C, ms, load_add(tile(j)), None, s_ref, p_ref)

    def body(i, carry):
        ms, acc, alpha_prev = carry
        a = 2 * i
        s_ref[1] = logits(a + 1)
        ms, alpha_a = softmax(0, a, ms)
        acc = acc * alpha_prev + _dot(load_vt(tile(a - 1)), p_ref[1])
        s_ref[0] = logits(a + 2)
        ms, alpha_b = softmax(1, a + 1, ms)
        acc = acc * alpha_a + _dot(load_vt(tile(a)), p_ref[0])
        return ms, acc, alpha_b

    ms, acc = _attend_init(heads)
    ms, acc, alpha_prev = lax.fori_loop(0, pairs, body, (ms, acc, jnp.ones((1, heads * T), F32)))
    acc = acc * alpha_prev + _dot(load_vt(tile(2 * pairs - 1)), p_ref[1])
    return ms, acc


def _nsa_kernel(q_ref, kc_ref, vcx_ref, ks_ref, vst_ref, kw_ref, vwt_ref, gate_ref, band_ref, tdiag_ref,
                o_ref, sc_ref, sel_ref, s_ref, p_ref):
    qi = pl.program_id(2)
    hg, T, C = NSA_GROUP, Q_BLOCK, KEY_CHUNK
    t0 = qi * T
    n_cp = kc_ref.shape[2]
    n_s = sel_ref.shape[1]
    row_c = lax.broadcasted_iota(I32, (C, T), 0)
    tpos_c = t0 + lax.broadcasted_iota(I32, (C, T), 1)
    diag_start = pl.multiple_of(jnp.maximum(t0 - T, 0), T)

    band0 = pl.multiple_of(jnp.maximum(t0 // CMP_STRIDE - 2 * SUBLANES, 0), SUBLANES)
    sc_ref[...] = jnp.where(lax.broadcasted_iota(I32, sc_ref.shape, 0) >= band0 + CMP_BAND, NEG, 0.0)
    sc_ref[pl.ds(band0, CMP_BAND), :] = band_ref[0]
    rows_c = min(C, n_cp)
    cmp_tiles = [(kc_ref[0, 0, lo:lo + rows_c, :], vcx_ref[0, 0, :, lo:lo + rows_c], None,
                  sc_ref[lo:lo + rows_c, :]) for lo in range(0, n_cp, rows_c)]
    m_c, acc_c = _attend_tiles(_attend_init(hg, vcx_ref.shape[2]), q_ref, cmp_tiles, s_ref, p_ref)
    seen = jnp.concatenate(m_c, axis=1) > 0.5 * NEG
    acc_c = acc_c * jnp.where(seen, 1.0 / acc_c[HEAD_DIM:HEAD_DIM + 1], 0.0)
    o_c = [acc_c[:HEAD_DIM, h * T:(h + 1) * T] for h in range(hg)]
    imp = acc_c[V_ROWS:, 0:T]
    for h in range(1, hg):
        imp = imp + acc_c[V_ROWS:, h * T:(h + 1) * T]

    blk = lax.broadcasted_iota(I32, (n_s, T), 0)
    blk_f = blk.astype(F32)
    cur = (t0 + lax.broadcasted_iota(I32, (n_s, T), 1)) // SLC_LEN
    forced = (blk == 0) | (blk == cur) | (blk == cur - 1)
    work = jnp.where(forced, 1e6, jnp.where(blk <= cur, imp, -1e6))

    def pick(_, carry):
        work, sel = carry
        mx = jnp.max(work, axis=0, keepdims=True)
        first = jnp.min(jnp.where(work == mx, blk_f, float(n_s)), axis=0, keepdims=True)
        hit = blk_f == first
        return jnp.where(hit, -3e38, work), jnp.where(hit, 1.0, sel)

    _, sel = lax.fori_loop(0, min(SLC_TOPN, n_s), pick, (work, jnp.zeros((n_s, T), F32)))
    sel_add = jnp.where(sel > 0.5, 0.0, NEG)
    sel_ref[0] = sel_add
    sel_ref[1] = jnp.where(blk < (t0 - T) // SLC_LEN, sel_add, NEG)

    def block_mask(start, which):
        first = start // SLC_LEN
        return jnp.concatenate([jnp.broadcast_to(sel_ref[which, pl.ds(first + i, 1), :], (SLC_LEN, T))
                                for i in range(C // SLC_LEN)], axis=0)

    load_ks = lambda start: ks_ref[0, 0, pl.ds(start, C), :]
    load_vst = lambda start: vst_ref[0, 0, :, pl.ds(start, C)]
    carry = _far_attend(qi, hg, ks_ref.shape[2] // C, q_ref, load_ks, load_vst,
                        lambda start: block_mask(start, 1), s_ref, p_ref)
    last_tile = (load_ks(diag_start), load_vst(diag_start), block_mask(diag_start, 0), tdiag_ref[0])
    o_s = _attend_finish(_attend_tiles(carry, q_ref, [last_tile], s_ref, p_ref))

    def win_tile(lo, last):
        start = pl.multiple_of(jnp.maximum(lo, 0), T)
        pos = row_c + start
        rel = tpos_c - pos
        add = jnp.where((pos < lo + C) & (rel >= 0) & (rel < WINDOW), 0.0, NEG)
        return (kw_ref[0, 0, pl.ds(start, C), :], vwt_ref[0, 0, :, pl.ds(start, C)], add,
                tdiag_ref[0] if last else None)

    win_tiles = [win_tile(t0 - T - 2 * C, False), win_tile(t0 - T - C, False), win_tile(t0 - T, True)]
    o_w = _attend_finish(_attend_tiles(_attend_init(hg), q_ref, win_tiles, s_ref, p_ref))

    gates = gate_ref[0, 0]
    for h in range(hg):
        cols = slice(h * T, (h + 1) * T)
        o = (gates[3 * h:3 * h + 1] * o_c[h] + gates[3 * h + 1:3 * h + 2] * o_s[:, cols]
             + gates[3 * h + 2:3 * h + 3] * o_w[:, cols])
        o_ref[0, h] = o.T.astype(o_ref.dtype)


def _nsa(q, kc, vcx, ks, vst, kw, vwt, gates_t, band, tdiag):
    b, _, s, _ = q.shape
    g, hg = NSA_KV_HEADS, NSA_GROUP
    nq = s // Q_BLOCK
    n_cp = kc.shape[2]
    assert n_cp % min(KEY_CHUNK, n_cp) == 0
    rows = lambda r: pl.BlockSpec((1, 1, r, HEAD_DIM), lambda bi, gi, qi: (bi, gi, 0, 0))
    cols = lambda r, c: pl.BlockSpec((1, 1, r, c), lambda bi, gi, qi: (bi, gi, 0, 0))
    return pl.pallas_call(
        _nsa_kernel,
        grid=(b, g, nq),
        in_specs=[
            pl.BlockSpec((1, hg, Q_BLOCK, HEAD_DIM), lambda bi, gi, qi: (bi, gi, qi, 0)),
            rows(n_cp), cols(vcx.shape[2], n_cp), rows(s), cols(V_ROWS, s), rows(s), cols(V_ROWS, s),
            pl.BlockSpec((1, 1, hg * 3, Q_BLOCK), lambda bi, gi, qi: (bi, gi, 0, qi)),
            pl.BlockSpec((1, CMP_BAND, hg * Q_BLOCK), lambda bi, gi, qi: (jnp.minimum(qi, 2), 0, gi)),
            pl.BlockSpec((1, KEY_CHUNK, hg * Q_BLOCK), lambda bi, gi, qi: (jnp.minimum(qi, 1), 0, gi)),
        ],
        out_specs=pl.BlockSpec((1, hg, Q_BLOCK, HEAD_DIM), lambda bi, gi, qi: (bi, gi, qi, 0)),
        out_shape=jax.ShapeDtypeStruct((b, NSA_HEADS, s, HEAD_DIM), BF16),
        scratch_shapes=[pltpu.VMEM((n_cp, hg * Q_BLOCK), F32), pltpu.VMEM((2, s // SLC_LEN, Q_BLOCK), F32),
                        pltpu.VMEM((2, KEY_CHUNK, hg * Q_BLOCK), F32), pltpu.VMEM((2, KEY_CHUNK, hg * Q_BLOCK), BF16)],
        compiler_params=_params("parallel", "parallel", "arbitrary"),
        name="nsa",
    )(q, kc, vcx, ks, vst, kw, vwt, gates_t, band, tdiag)


DIGIT_BITS = 14
DIGIT_BIAS = 0x80
DIGIT_DTYPE = jnp.bfloat16


def _pack_patterns(p):
    half = p.shape[0] // 2
    return (p[:half] << 16) | p[half:]


def _sortable_key(x):
    b = lax.bitcast_convert_type(jnp.where(x == 0.0, 0.0, x), I32)
    return b ^ ((b >> 31) & 0x7FFFFFFF)


def _dsa_kernel(q_ref, iq_ref, iw_ref, ik_ref, k_ref, vt_ref, tdiag_ref, o_ref, key_ref, dig_ref, s_ref, p_ref,
                *, n_keep):
    qi = pl.program_id(1)
    H, T, C = DSA_HEADS, Q_BLOCK, KEY_CHUNK
    t0 = qi * T
    n_chunks = qi // 2 + 1
    iw = iw_ref[0]
    row = lax.broadcasted_iota(I32, (C, T), 0)
    tpos = t0 + lax.broadcasted_iota(I32, (C, T), 1)
    int_min = jnp.iinfo(jnp.int32).min

    def score_step(j, _):
        for u in range(2):
            start = pl.multiple_of((2 * j + u) * C, C)
            ikc = ik_ref[0, pl.ds(start, C), :]
            score = jnp.maximum(_dot_nt(ikc, iq_ref[0, 0]), 0.0) * iw[0:1]
            for h in range(1, IDX_HEADS):
                score = score + jnp.maximum(_dot_nt(ikc, iq_ref[0, h]), 0.0) * iw[h:h + 1]
            score = jnp.where(row + start <= tpos, score, NEG)
            key = _sortable_key(score)
            key_ref[pl.ds(start, C), :] = key
            digit = lax.shift_right_logical(key ^ int_min, jnp.int32(32 - DIGIT_BITS))
            dig_ref[pl.ds(pl.multiple_of(start // 2, C // 2), C // 2), :] = _pack_patterns(digit + DIGIT_BIAS)
        return 0

    lax.fori_loop(0, (n_chunks + 1) // 2, score_step, 0)

    def count_digits(cand_digit):
        cand = cand_digit + DIGIT_BIAS
        cand = pltpu.bitcast(jnp.broadcast_to((cand << 16) | cand, (SUBLANES, T)), DIGIT_DTYPE)

        def tile_hits(j):
            words = dig_ref[pl.ds(pl.multiple_of(j * (C // 2), C // 2), C // 2), :]
            vals = pltpu.bitcast(words, DIGIT_DTYPE).reshape(C // (2 * SUBLANES), 2 * SUBLANES, T)
            hits = jnp.where(vals >= cand[None], jnp.ones((), DIGIT_DTYPE), jnp.zeros((), DIGIT_DTYPE))
            parts = [hits[i] for i in range(hits.shape[0])]
            while len(parts) > 1:
                parts = [parts[i] + parts[i + 1] for i in range(0, len(parts), 2)]
            return parts[0].astype(F32)

        def body(i, acc):
            second = 2 * i + 1
            keep = jnp.where(second < n_chunks, 1.0, 0.0)
            return acc + tile_hits(2 * i) + tile_hits(jnp.minimum(second, n_chunks - 1)) * keep

        acc = lax.fori_loop(0, (n_chunks + 1) // 2, body, jnp.zeros((2 * SUBLANES, T), F32))
        return jnp.sum(acc, axis=0, keepdims=True).astype(I32)

    def digit_search(need, n_all):
        def bit(i, carry):
            digit, n_at = carry
            cand = digit | lax.shift_left(jnp.int32(1), DIGIT_BITS - 1 - i)
            n_ge = count_digits(cand)
            ok = n_ge >= need
            return jnp.where(ok, cand, digit), jnp.where(ok, n_ge, n_at)

        return lax.fori_loop(0, DIGIT_BITS, bit, (jnp.zeros((1, T), I32), n_all))

    def count(pred_fn):
        def body(j, acc):
            start = pl.multiple_of(j * C, C)
            hits = jnp.where(pred_fn(key_ref[pl.ds(start, C), :], start), 1.0, 0.0)
            return acc + jnp.sum(hits.reshape(C // COUNT_ROWS, COUNT_ROWS, T), axis=0)
        acc = lax.fori_loop(0, n_chunks, body, jnp.zeros((COUNT_ROWS, T), F32))
        return jnp.sum(acc, axis=0, keepdims=True).astype(I32)

    d1, n_at1 = digit_search(n_keep, jnp.full((1, T), 1, I32) * (n_chunks * C))
    n_gt1 = count_digits(d1 + 1)

    def second_digits(j, _):
        start = pl.multiple_of(j * C, C)
        u = key_ref[pl.ds(start, C), :] ^ int_min
        digit = lax.shift_right_logical(u, jnp.int32(32 - 2 * DIGIT_BITS)) & (2 ** DIGIT_BITS - 1)
        first = lax.shift_right_logical(u, jnp.int32(32 - DIGIT_BITS))
        pattern = jnp.where(first == d1, digit + DIGIT_BIAS, 0)
        dig_ref[pl.ds(pl.multiple_of(start // 2, C // 2), C // 2), :] = _pack_patterns(pattern)
        return 0

    lax.fori_loop(0, n_chunks, second_digits, 0)
    d2, n_at2 = digit_search(n_keep - n_gt1, n_at1 - n_gt1)

    def thr_bit(i, carry):
        thr_u, n_at = carry
        cand_u = thr_u | lax.shift_left(jnp.int32(1), 31 - i)
        cand = cand_u ^ int_min
        n_ge = count(lambda keys, _: keys >= cand)
        ok = n_ge >= n_keep
        return jnp.where(ok, cand_u, thr_u), jnp.where(ok, n_ge, n_at)

    prefix = lax.shift_left(d1, jnp.int32(32 - DIGIT_BITS)) | lax.shift_left(d2, jnp.int32(32 - 2 * DIGIT_BITS))
    thr_u, n_at = lax.fori_loop(2 * DIGIT_BITS, 32, thr_bit, (prefix, n_gt1 + n_at2))
    thr = thr_u ^ int_min

    @pl.when(jnp.max((n_at - n_keep).astype(F32)) > 0.0)
    def _():
        need = n_keep - count(lambda keys, _: keys > thr)
        n_pos_bits = int(math.ceil(math.log2(key_ref.shape[0] + 1)))

        def pos_bit(i, lim):
            cand = lim + lax.shift_left(jnp.int32(1), n_pos_bits - 1 - i)
            n_tied = count(lambda keys, start: (keys == thr) & (row + start < cand))
            return jnp.where(n_tied <= need, cand, lim)

        lim = lax.fori_loop(0, n_pos_bits, pos_bit, jnp.zeros((1, T), I32))

        def drop(j, _):
            start = pl.multiple_of(j * C, C)
            keys = key_ref[pl.ds(start, C), :]
            key_ref[pl.ds(start, C), :] = jnp.where((keys == thr) & (row + start >= lim), int_min, keys)
            return 0

        lax.fori_loop(0, n_chunks, drop, 0)

    def key_mask(start, far):
        keep = key_ref[pl.ds(start, C), :] >= thr
        if far:
            keep = keep & (row + start < t0 - T)
        return jnp.where(keep, 0.0, NEG)

    load_k = lambda start: k_ref[0, pl.ds(start, C), :]
    load_vt = lambda start: vt_ref[0, :, pl.ds(start, C)]
    carry = _far_attend(qi, H, k_ref.shape[1] // C, q_ref, load_k, load_vt, lambda start: key_mask(start, True), s_ref, p_ref)
    diag_start = pl.multiple_of(jnp.maximum(t0 - T, 0), T)
    last_tile = (load_k(diag_start), load_vt(diag_start), key_mask(diag_start, False), tdiag_ref[0])
    o = _attend_finish(_attend_tiles(carry, q_ref, [last_tile], s_ref, p_ref))
    for h in range(H):
        o_ref[0, h] = o[:, h * T:(h + 1) * T].T.astype(o_ref.dtype)


def _dsa(q, iq, iw_t, ik, k, vt, tdiag):
    b, _, s, _ = q.shape
    assert s % (2 * KEY_CHUNK) == 0
    nq = s // Q_BLOCK
    n_keep = min(DSA_TOPK_MAX, s // 4)
    heads = lambda: pl.BlockSpec((1, DSA_HEADS, Q_BLOCK, HEAD_DIM), lambda bi, qi: (bi, 0, qi, 0))
    rows = lambda: pl.BlockSpec((1, s, HEAD_DIM), lambda bi, qi: (bi, 0, 0))
    return pl.pallas_call(
        functools.partial(_dsa_kernel, n_keep=n_keep),
        grid=(b, nq),
        in_specs=[heads(), heads(),
                  pl.BlockSpec((1, IDX_HEADS, Q_BLOCK), lambda bi, qi: (bi, 0, qi)),
                  rows(), rows(),
                  pl.BlockSpec((1, V_ROWS, s), lambda bi, qi: (bi, 0, 0)),
                  pl.BlockSpec((1, KEY_CHUNK, DSA_HEADS * Q_BLOCK), lambda bi, qi: (jnp.minimum(qi, 1), 0, 1))],
        out_specs=heads(),
        out_shape=jax.ShapeDtypeStruct((b, DSA_HEADS, s, HEAD_DIM), BF16),
        scratch_shapes=[pltpu.VMEM((s, Q_BLOCK), I32), pltpu.VMEM((s // 2, Q_BLOCK), I32),
                        pltpu.VMEM((2, KEY_CHUNK, DSA_HEADS * Q_BLOCK), F32),
                        pltpu.VMEM((2, KEY_CHUNK, DSA_HEADS * Q_BLOCK), BF16)],
        compiler_params=_params("parallel", "arbitrary"),
        name="dsa",
    )(q, iq, iw_t, ik, k, vt, tdiag)


def _merge_kernel(x_ref, oa_ref, ob_ref, ga_ref, gb_ref, wa_ref, wb_ref, wo_ref, gn_ref, wr_ref, br_ref,
                  x1_ref, xn_ref, gate_ref, gate_t_ref):
    def branch(o_ref, w_ref):
        acc = _dot(o_ref[0, 0], w_ref[0])
        for h in range(1, o_ref.shape[1]):
            acc = acc + _dot(o_ref[0, h], w_ref[h])
        return acc

    merged = ga_ref[...] * branch(oa_ref, wa_ref) + gb_ref[...] * branch(ob_ref, wb_ref)
    x1 = x_ref[...] + _dot(merged.astype(BF16), wo_ref[...])
    x1_ref[...] = x1
    xn = _row_rms(x1, gn_ref[...]).astype(BF16)
    xn_ref[...] = xn

    logits = _dot(xn, wr_ref[...]) + br_ref[...]
    lane = lax.broadcasted_iota(I32, logits.shape, 1)
    work = jnp.where(lane < N_EXPERTS, logits, -3e38)
    gates = jnp.zeros_like(logits)
    denom = jnp.zeros((logits.shape[0], 1), F32)
    top = None
    for _ in range(MOE_TOPK):
        mx = jnp.max(work, axis=1, keepdims=True)
        first = jnp.min(jnp.where(work == mx, lane, LANES), axis=1, keepdims=True)
        hit = lane == first
        top = mx if top is None else top
        e = jnp.exp(mx - top)
        gates = jnp.where(hit, e, gates)
        denom = denom + e
        work = jnp.where(hit, -3e38, work)
    gates = gates / denom
    gate_ref[...] = gates
    gate_t_ref[...] = gates.T


def _merge(x, oa, ob, ga, gb, w_branch_a, w_branch_b, w_out, ffn_norm, w_router, b_router, tm):
    b, s, d = x.shape
    n = b * s
    xf = x.reshape(n, d)
    nt = s // tm
    wa = w_branch_a.reshape(NSA_HEADS, HEAD_DIM, d).astype(BF16)
    wb = w_branch_b.reshape(DSA_HEADS, HEAD_DIM, d).astype(BF16)
    wr = jnp.concatenate([w_router, jnp.zeros((d, LANES - N_EXPERTS), F32)], axis=1).astype(BF16)
    br = jnp.concatenate([b_router, jnp.zeros((LANES - N_EXPERTS,), F32)]).reshape(1, LANES)
    consts = [wa, wb, w_out.astype(BF16), ffn_norm.reshape(1, d), wr, br]
    row = lambda w: pl.BlockSpec((tm, w), lambda bi, ti: (bi * nt + ti, 0))
    heads = lambda: pl.BlockSpec((1, NSA_HEADS, tm, HEAD_DIM), lambda bi, ti: (bi, 0, ti, 0))
    return pl.pallas_call(
        _merge_kernel,
        grid=(b, nt),
        in_specs=[row(d), heads(), heads(), row(d), row(d)] + [_const_spec(c.shape) for c in consts],
        out_specs=[row(d), row(d), row(LANES), pl.BlockSpec((LANES, tm), lambda bi, ti: (0, bi * nt + ti))],
        out_shape=[jax.ShapeDtypeStruct((n, d), F32), jax.ShapeDtypeStruct((n, d), BF16),
                   jax.ShapeDtypeStruct((n, LANES), F32), jax.ShapeDtypeStruct((LANES, n), F32)],
        compiler_params=_params("parallel", "parallel"),
        name="merge",
    )(xf, oa, ob, ga, gb, *consts)


def _deinterleave_index(width):
    half = MXU_DIM // 2
    c = np.arange(width)
    r = c % MXU_DIM
    return (c - r) + np.where(r < half, 2 * r, 2 * (r - half) + 1)


def _moe_prep_kernel(w_ref, perm_ref, o_ref):
    perm = perm_ref[...]
    for g in range(w_ref.shape[2] // MXU_DIM):
        cols = slice(g * MXU_DIM, (g + 1) * MXU_DIM)
        o_ref[0, :, cols] = _dot(w_ref[0, :, cols].astype(BF16), perm).astype(BF16)


def _moe_prep(w_gate_up, tr):
    e, d, f2 = w_gate_up.shape
    src = _deinterleave_index(MXU_DIM)
    perm = jnp.asarray((np.arange(MXU_DIM)[:, None] == src[None, :]).astype(np.float32), BF16)
    return pl.pallas_call(
        _moe_prep_kernel,
        grid=(e, d // tr),
        in_specs=[pl.BlockSpec((1, tr, f2), lambda i, j: (i, j, 0)), _const_spec(perm.shape)],
        out_specs=pl.BlockSpec((1, tr, f2), lambda i, j: (i, j, 0)),
        out_shape=jax.ShapeDtypeStruct((e, d, f2), BF16),
        compiler_params=_params("parallel", "parallel"),
        name="moe_prep",
    )(w_gate_up, perm)


def _moe_kernel(x1_ref, xn_ref, gate_ref, gate_t_ref, lower_ref, upper_ref, wgu_ref, bgu_ref, wd_ref, bd_ref,
                o_ref, rank_ref, rank_t_ref):
    e = pl.program_id(1)
    tm, R = xn_ref.shape[0], MOE_ROWS

    @pl.when(e == 0)
    def _():
        o_ref[...] = x1_ref[...]
        routed = jnp.where(gate_ref[...] > 0.0, 1.0, 0.0).astype(BF16)
        routed_t = jnp.where(gate_t_ref[...] > 0.0, 1.0, 0.0).astype(BF16)
        rank_ref[...] = _dot(lower_ref[...], routed)
        rank_t_ref[...] = _dot(routed_t, upper_ref[...])

    g_row = gate_t_ref[pl.ds(e, 1), :]
    is_e = lax.broadcasted_iota(I32, (tm, LANES), 1) == e
    g_col = jnp.sum(jnp.where(is_e, gate_ref[...], 0.0), axis=1, keepdims=True)
    rank_row = jnp.where(g_row > 0.0, rank_t_ref[pl.ds(e, 1), :], -1.0)
    rank_col = jnp.where(g_col > 0.0, jnp.sum(jnp.where(is_e, rank_ref[...], 0.0), axis=1, keepdims=True), -1.0)
    n_routed = jnp.sum(jnp.where(g_row > 0.0, 1.0, 0.0)).astype(I32)
    slot_r = lax.broadcasted_iota(I32, (R, tm), 0).astype(F32)
    slot_c = lax.broadcasted_iota(I32, (tm, R), 1).astype(F32)
    half = MXU_DIM // 2

    def block(j, _):
        base = (j * R).astype(F32)
        pick = jnp.where(rank_row - base == slot_r, 1.0, 0.0)
        xg = _dot(pick.astype(BF16), xn_ref[...]).astype(BF16)
        hu = _dot(xg, wgu_ref[0]) + bgu_ref[0]
        acts = []
        for g in range(hu.shape[1] // MXU_DIM):
            glu = jnp.minimum(hu[:, g * MXU_DIM:g * MXU_DIM + half], SWIGLU_LIMIT)
            lin = jnp.clip(hu[:, g * MXU_DIM + half:(g + 1) * MXU_DIM], -SWIGLU_LIMIT, SWIGLU_LIMIT)
            acts.append((glu * jax.nn.sigmoid(SWIGLU_ALPHA * glu) * (lin + 1.0)).astype(BF16))
        y = _dot(jnp.concatenate(acts, axis=1), wd_ref[0]) + bd_ref[0]
        g_r = jnp.sum(pick * g_row, axis=1, keepdims=True)
        put = jnp.where(rank_col - base == slot_c, 1.0, 0.0).astype(BF16)
        o_ref[...] += _dot(put, (y * g_r).astype(BF16))
        return 0

    lax.fori_loop(0, (n_routed + R - 1) // R, block, 0)


def _moe(x1, xn, gates, gates_t, w_gate_up, b_gate_up, w_down, b_down, tm):
    n, d = x1.shape
    f = D_EXPERT
    wgu = _moe_prep(w_gate_up, tr=256)
    bgu = b_gate_up[:, _deinterleave_index(2 * f)].reshape(N_EXPERTS, 1, 2 * f)
    wd = w_down.astype(BF16)
    bd = b_down.reshape(N_EXPERTS, 1, d)
    earlier = np.arange(tm)[None, :] < np.arange(tm)[:, None]
    lower = jnp.asarray(earlier.astype(np.float32), BF16)
    upper = jnp.asarray(earlier.T.astype(np.float32), BF16)
    row = lambda w: pl.BlockSpec((tm, w), lambda i, e: (i, 0))
    per_e = lambda a, b_: pl.BlockSpec((1, a, b_), lambda i, e: (e, 0, 0))
    return pl.pallas_call(
        _moe_kernel,
        grid=(n // tm, N_EXPERTS),
        in_specs=[row(d), row(d), row(LANES), pl.BlockSpec((LANES, tm), lambda i, e: (0, i)),
                  _const_spec(lower.shape), _const_spec(upper.shape),
                  per_e(d, 2 * f), per_e(1, 2 * f), per_e(f, d), per_e(1, d)],
        out_specs=row(d),
        out_shape=jax.ShapeDtypeStruct((n, d), F32),
        scratch_shapes=[pltpu.VMEM((tm, LANES), F32), pltpu.VMEM((LANES, tm), F32)],
        compiler_params=_params("parallel", "arbitrary"),
        name="moe",
    )(x1, xn, gates, gates_t, lower, upper, wgu, bgu, wd, bd)


def _rel_bucket(rel):
    n = jnp.maximum(rel, 0)
    exact = REL_BUCKETS // 2
    nf = jnp.maximum(n, 1).astype(F32)
    large = exact + (jnp.log(nf / exact) / math.log(REL_MAX_DIST / exact) * (REL_BUCKETS - exact)).astype(I32)
    large = jnp.minimum(large, REL_BUCKETS - 1)
    return jnp.where(n < exact, n, large)


def _rel_table(rel_bias, rel):
    far = rel_bias[REL_BUCKETS - 1]
    tab = jnp.where(rel[..., None] >= 0, (rel_bias[_rel_bucket(rel)] - far) * LOG2E, NEG)
    tab = jnp.moveaxis(tab, -1, -2)
    return tab.reshape(tab.shape[:-2] + (-1,)).astype(F32)


def _bias_tables(rel_bias):
    i = np.arange(Q_BLOCK)[None, :]
    j = np.arange(KEY_CHUNK)[:, None]
    tdiag = _rel_table(rel_bias, jnp.asarray(np.stack([i - j, i - j + Q_BLOCK])))
    r = np.arange(CMP_BAND)[:, None]
    rel_band = np.stack([i - (CMP_LEN - 1) - CMP_STRIDE * (c0 + r) for c0 in (0, -SUBLANES, -2 * SUBLANES)])
    band = _rel_table(rel_bias[:, :NSA_HEADS], jnp.asarray(rel_band))
    return tdiag, band


def _overlap_matrix_t(s):
    n_cp = s // CMP_STRIDE
    n_s = s // SLC_LEN
    c_start = np.arange(n_cp) * CMP_STRIDE
    s_start = np.arange(n_s) * SLC_LEN
    ovl = (c_start[None, :] < s_start[:, None] + SLC_LEN) & (c_start[None, :] + CMP_LEN > s_start[:, None])
    ovl[:, n_cp - 1] = False
    return jnp.asarray(ovl.astype(np.float32), BF16)


def _layer(x, attn_norm, w_in, nsa_q_norm, nsa_k_norm, cmp_k_pe, cmp_k_w1, cmp_k_b1, cmp_k_w2,
           cmp_v_pe, cmp_v_w1, cmp_v_b1, cmp_v_w2, dsa_kv_norm, dsa_w_uk, dsa_w_uv, dsa_q_norm,
           dsa_k_norm, w_branch_a, w_branch_b, w_out, rel_bias, ffn_norm, w_router, b_router,
           w_gate_up, b_gate_up, w_down, b_down):
    b, s, d = x.shape
    n = b * s
    g = NSA_KV_HEADS
    (qa, ck, cv, ks, sv_t, kw, wv_t, misc_t, qb, kb, vb_t, iq, ik, ga, gb) = _in_proj(
        x, attn_norm, w_in, nsa_q_norm, nsa_k_norm, dsa_q_norm, dsa_kv_norm, dsa_w_uk, dsa_w_uv, dsa_k_norm,
        tm=min(256, s))

    chunks = lambda a: a.reshape(b, s // CMP_STRIDE, CMP_STRIDE, g, HEAD_DIM).transpose(0, 3, 1, 2, 4).reshape(
        b, g, s // CMP_STRIDE, CMP_STRIDE * HEAD_DIM)
    kc = _compress(chunks(ck), cmp_k_pe, cmp_k_w1, cmp_k_b1, cmp_k_w2, nsa_k_norm, True)
    vc = _compress(chunks(cv), cmp_v_pe, cmp_v_w1, cmp_v_b1, cmp_v_w2, nsa_k_norm, False)

    tdiag, band = _bias_tables(rel_bias)
    gates_t = misc_t[:, :NSA_HEADS * 3].reshape(b, g, NSA_GROUP * 3, s)
    overlap_t = _overlap_matrix_t(s)
    vc_t = vc.transpose(0, 1, 3, 2)
    vcx = jnp.concatenate([vc_t, jnp.ones((b, g, V_ROWS - HEAD_DIM, vc_t.shape[3]), BF16),
                           jnp.broadcast_to(overlap_t, (b, g) + overlap_t.shape)], axis=2)
    oa = _nsa(qa, kc, vcx, ks, sv_t, kw, wv_t, gates_t, band, tdiag)

    iw_t = misc_t[:, NSA_HEADS * 3:NSA_HEADS * 3 + IDX_HEADS]
    ob = _dsa(qb, iq, iw_t, ik.reshape(b, s, IDX_DIM), kb.reshape(b, s, HEAD_DIM), vb_t, tdiag)

    x1, xn, moe_gates, moe_gates_t = _merge(x, oa, ob, ga, gb, w_branch_a, w_branch_b, w_out, ffn_norm, w_router,
                               b_router, tm=min(512, s))
    out = _moe(x1, xn, moe_gates, moe_gates_t, w_gate_up, b_gate_up, w_down, b_down, tm=min(MOE_TOKENS, n))
    return out.reshape(b, s, d)


def kernel(x, attn_norm, w_in, nsa_q_norm, nsa_k_norm, cmp_k_pe, cmp_k_w1, cmp_k_b1, cmp_k_w2, cmp_v_pe, cmp_v_w1, cmp_v_b1, cmp_v_w2, dsa_kv_norm, dsa_w_uk, dsa_w_uv, dsa_q_norm, dsa_k_norm, w_branch_a, w_branch_b, w_out, rel_bias, ffn_norm, w_router, b_router, w_gate_up, b_gate_up, w_down, b_down):
    for l in range(attn_norm.shape[0]):
        x = _layer(x, attn_norm[l], w_in[l], nsa_q_norm[l], nsa_k_norm[l], cmp_k_pe[l], cmp_k_w1[l],
                   cmp_k_b1[l], cmp_k_w2[l], cmp_v_pe[l], cmp_v_w1[l], cmp_v_b1[l], cmp_v_w2[l],
                   dsa_kv_norm[l], dsa_w_uk[l], dsa_w_uv[l], dsa_q_norm[l], dsa_k_norm[l],
                   w_branch_a[l], w_branch_b[l], w_out[l], rel_bias, ffn_norm[l], w_router[l],
                   b_router[l], w_gate_up[l], b_gate_up[l], w_down[l], b_down[l])
    return x
```

```python
import functools
import math

import numpy as np
import jax
import jax.numpy as jnp
from jax import lax
from jax.experimental import pallas as pl
from jax.experimental.pallas import tpu as pltpu

D_MODEL = 1024
HEAD_DIM = 64
NSA_HEADS = 8
NSA_KV_HEADS = 2
NSA_GROUP = NSA_HEADS // NSA_KV_HEADS
CMP_LEN = 32
CMP_STRIDE = 16
CMP_HID = 128
SLC_LEN = 64
SLC_TOPN = 16
WINDOW = 512
DSA_HEADS = 8
DSA_KV_RANK = 128
IDX_HEADS = 8
IDX_DIM = 64
DSA_TOPK_MAX = 256
N_EXPERTS = 32
MOE_TOPK = 4
D_EXPERT = 1024
SWIGLU_LIMIT = 7.0
SWIGLU_ALPHA = 1.702
REL_BUCKETS = 32
REL_MAX_DIST = 128
NORM_EPS = 1e-6
NEG = -1e30
LOG2E = math.log2(math.e)

Q_BLOCK = 128
KEY_CHUNK = 256
N_FORCED = 3
CMP_BAND = 24
COUNT_ROWS = 32
LANES = 128
SUBLANES = 8
MXU_DIM = 256
V_ROWS = HEAD_DIM + SUBLANES
MOE_TOKENS = 1024
MOE_ROWS = 160
VMEM_LIMIT = 56 * 1024 * 1024

F32 = jnp.float32
BF16 = jnp.bfloat16
I32 = jnp.int32


def _params(*sem):
    return pltpu.CompilerParams(dimension_semantics=sem, vmem_limit_bytes=VMEM_LIMIT)


def _dot(a, b):
    return jnp.dot(a, b, preferred_element_type=F32)


def _dot_nt(a, b):
    return lax.dot_general(a, b, (((1,), (1,)), ((), ())), preferred_element_type=F32)


def _const_spec(shape):
    nd = len(shape)
    return pl.BlockSpec(shape, lambda *_: (0,) * nd)


def _segment_sumsq(y, seg_ones):
    y2 = y * y
    hi = y2.astype(BF16)
    lo = (y2 - hi.astype(F32)).astype(BF16)
    return _dot(hi, seg_ones) + _dot(lo, seg_ones)


def _head_rms(y, seg_ones, gain):
    ms = _segment_sumsq(y, seg_ones) * (1.0 / HEAD_DIM)
    return y * lax.rsqrt(ms + NORM_EPS) * gain


def _row_rms(y, gain):
    ms = jnp.mean(y * y, axis=-1, keepdims=True)
    return y * lax.rsqrt(ms + NORM_EPS) * gain


def _in_proj_kernel(x_ref, g_ref, wqa_ref, wkv_ref, wmisc_ref, wqb_ref, wckv_ref, wiq_ref, wik_ref,
                    wga_ref, wgb_ref, seg512_ref, seg128_ref, gqa_ref, gka_ref, gqb_ref, gckv_ref,
                    wuk_ref, wuv_ref, gkb_ref,
                    qa_ref, ck_ref, cv_ref, ks_ref, sv_ref, kw_ref, wv_ref, misc_ref, qb_ref,
                    kb_ref, vb_ref, iq_ref, ik_ref, ga_ref, gb_ref):
    h = _row_rms(x_ref[...], g_ref[...]).astype(BF16)
    scale = HEAD_DIM ** -0.5 * LOG2E
    seg512 = seg512_ref[...]
    seg128 = seg128_ref[...]

    def put_heads(o_ref, y):
        for i in range(o_ref.shape[1]):
            o_ref[0, i] = y[:, i * HEAD_DIM:(i + 1) * HEAD_DIM].astype(o_ref.dtype)

    def put_values_t(o_ref, y):
        y_t = y.T
        ones = jnp.ones((V_ROWS - HEAD_DIM, y.shape[0]), o_ref.dtype)
        for i in range(o_ref.shape[1]):
            o_ref[0, i, :HEAD_DIM, :] = y_t[i * HEAD_DIM:(i + 1) * HEAD_DIM].astype(o_ref.dtype)
            o_ref[0, i, HEAD_DIM:, :] = ones

    qa = _head_rms(_dot(h, wqa_ref[...]), seg512, gqa_ref[...])
    put_heads(qa_ref, qa * scale)

    kv = _dot(h, wkv_ref[...])
    ck_ref[...] = kv[:, 0:128]
    cv_ref[...] = kv[:, 128:256]
    put_heads(ks_ref, _head_rms(kv[:, 256:384], seg128, gka_ref[...]))
    put_values_t(sv_ref, kv[:, 384:512])
    put_heads(kw_ref, _head_rms(kv[:, 512:640], seg128, gka_ref[...]))
    put_values_t(wv_ref, kv[:, 640:768])

    misc = _dot(h, wmisc_ref[...])
    lane = lax.broadcasted_iota(I32, misc.shape, 1)
    iw_scale = (IDX_HEADS ** -0.5)
    idx_scale = IDX_DIM ** -0.5
    misc = jnp.where(lane < NSA_HEADS * 3, jax.nn.sigmoid(misc), (misc * iw_scale) * idx_scale)
    misc_ref[0] = misc.T[:misc_ref.shape[1]]

    qb = _head_rms(_dot(h, wqb_ref[...]), seg512, gqb_ref[...])
    put_heads(qb_ref, qb * scale)

    ckv = _row_rms(_dot(h, wckv_ref[...]), gckv_ref[...]).astype(BF16)
    kb_ref[...] = _row_rms(_dot(ckv, wuk_ref[...]), gkb_ref[...]).astype(BF16)
    vb = _dot(ckv, wuv_ref[...])
    vb_ref[0, :HEAD_DIM, :] = vb.T.astype(BF16)
    vb_ref[0, HEAD_DIM:, :] = jnp.ones((V_ROWS - HEAD_DIM, vb.shape[0]), BF16)

    put_heads(iq_ref, _dot(h, wiq_ref[...]))
    ik_ref[...] = _dot(h, wik_ref[...]).astype(BF16)
    ga_ref[...] = jax.nn.sigmoid(_dot(h, wga_ref[...]))
    gb_ref[...] = jax.nn.sigmoid(_dot(h, wgb_ref[...]))


def _in_proj(x, attn_norm, w_in, nsa_q_norm, nsa_k_norm, dsa_q_norm, dsa_kv_norm, dsa_w_uk, dsa_w_uv,
             dsa_k_norm, tm):
    b, s, _ = x.shape
    n = b * s
    xf = x.reshape(n, -1)
    kvw = NSA_KV_HEADS * HEAD_DIM
    widths = (NSA_HEADS * HEAD_DIM, kvw, kvw, kvw, kvw, kvw, kvw, NSA_HEADS * 3,
              DSA_HEADS * HEAD_DIM, DSA_KV_RANK, IDX_HEADS * IDX_DIM, IDX_DIM, IDX_HEADS, D_MODEL, D_MODEL)
    offs = np.concatenate([[0], np.cumsum(widths)])
    sec = [w_in[:, offs[i]:offs[i + 1]] for i in range(len(widths))]
    wb = lambda w: w.astype(BF16)
    wqa = wb(sec[0])
    wkv = wb(jnp.concatenate(sec[1:7], axis=1))
    wmisc = wb(jnp.concatenate([sec[7], sec[12], jnp.zeros((D_MODEL, LANES - 32), F32)], axis=1))
    wqb, wckv, wiq, wik, wga, wgb = wb(sec[8]), wb(sec[9]), wb(sec[10]), wb(sec[11]), wb(sec[13]), wb(sec[14])

    def seg_ones(w):
        i = np.arange(w) // HEAD_DIM
        return jnp.asarray((i[:, None] == i[None, :]).astype(np.float32), BF16)

    consts = [attn_norm.reshape(1, -1), wqa, wkv, wmisc, wqb, wckv, wiq, wik, wga, wgb,
              seg_ones(512), seg_ones(128),
              jnp.tile(nsa_q_norm, NSA_HEADS).reshape(1, -1), jnp.tile(nsa_k_norm, NSA_KV_HEADS).reshape(1, -1),
              jnp.tile(dsa_q_norm, DSA_HEADS).reshape(1, -1), dsa_kv_norm.reshape(1, -1),
              wb(dsa_w_uk), wb(dsa_w_uv), dsa_k_norm.reshape(1, -1)]
    nt = s // tm
    g = NSA_KV_HEADS
    rows = lambda w, dt: (jax.ShapeDtypeStruct((n, w), dt), pl.BlockSpec((tm, w), lambda bi, ti: (bi * nt + ti, 0)))
    heads = lambda hh: (jax.ShapeDtypeStruct((b, hh, s, HEAD_DIM), BF16),
                        pl.BlockSpec((1, hh, tm, HEAD_DIM), lambda bi, ti: (bi, 0, ti, 0)))
    values_t = (jax.ShapeDtypeStruct((b, g, V_ROWS, s), BF16),
                pl.BlockSpec((1, g, V_ROWS, tm), lambda bi, ti: (bi, 0, 0, ti)))
    cols = lambda r, dt: (jax.ShapeDtypeStruct((b, r, s), dt), pl.BlockSpec((1, r, tm), lambda bi, ti: (bi, 0, ti)))
    outs = [heads(NSA_HEADS), rows(128, F32), rows(128, F32), heads(g), values_t, heads(g), values_t,
            cols(32, F32), heads(DSA_HEADS), rows(HEAD_DIM, BF16), cols(V_ROWS, BF16), heads(IDX_HEADS),
            rows(IDX_DIM, BF16), rows(D_MODEL, F32), rows(D_MODEL, F32)]
    return pl.pallas_call(
        _in_proj_kernel,
        grid=(b, nt),
        in_specs=[rows(D_MODEL, F32)[1]] + [_const_spec(c.shape) for c in consts],
        out_specs=[spec for _, spec in outs],
        out_shape=[shape for shape, _ in outs],
        compiler_params=_params("parallel", "parallel"),
        name="in_proj",
    )(xf, *consts)


def _compress_kernel(c_ref, pea_ref, peb_ref, w1a_ref, w1b_ref, b1_ref, w2_ref, gain_ref, o_ref, *, norm, n_c):
    c = c_ref[0, 0]
    ua = _dot((c + pea_ref[...]).astype(BF16), w1a_ref[...])
    ub = _dot((c + peb_ref[...]).astype(BF16), w1b_ref[...])
    rows = c.shape[0]
    ub_next = pltpu.roll(ub, rows - 1, 0)
    hid = jax.nn.gelu(ua + ub_next + b1_ref[...])
    o = _dot(hid.astype(BF16), w2_ref[...])
    if norm:
        o = _row_rms(o, gain_ref[...])
    ridx = lax.broadcasted_iota(I32, o.shape, 0)
    o_ref[0, 0] = jnp.where(ridx < n_c, o, 0.0).astype(o_ref.dtype)


def _compress(c, pe, w1, b1, w2, gain, norm):
    b, g, rows, width = c.shape
    half = CMP_STRIDE * HEAD_DIM
    consts = [pe[:CMP_STRIDE].reshape(1, half), pe[CMP_STRIDE:].reshape(1, half),
              w1[:half].astype(BF16), w1[half:].astype(BF16), b1.reshape(1, -1), w2.astype(BF16),
              gain.reshape(1, -1)]
    return pl.pallas_call(
        functools.partial(_compress_kernel, norm=norm, n_c=rows - 1),
        grid=(b, g),
        in_specs=[pl.BlockSpec((1, 1, rows, width), lambda i, j: (i, j, 0, 0))] + [_const_spec(x.shape) for x in consts],
        out_specs=pl.BlockSpec((1, 1, rows, HEAD_DIM), lambda i, j: (i, j, 0, 0)),
        out_shape=jax.ShapeDtypeStruct((b, g, rows, HEAD_DIM), BF16),
        compiler_params=_params("parallel", "parallel"),
        name="compress_norm" if norm else "compress",
    )(c, *consts)


def _softmax_tile(slot, rows, ms, add, table, s_ref, p_ref):
    T = Q_BLOCK
    new_m, alphas = [], []
    for h in range(len(ms)):
        cols = slice(h * T, (h + 1) * T)
        s = s_ref[slot, :rows, cols]
        if add is not None:
            s = s + add
        if table is not None:
            s = s + table[:, cols]
        m_new = jnp.maximum(ms[h], jnp.max(s, axis=0, keepdims=True))
        alphas.append(jnp.exp2(ms[h] - m_new))
        new_m.append(m_new)
        p_ref[slot, :rows, cols] = jnp.exp2(s - m_new).astype(p_ref.dtype)
    return new_m, jnp.concatenate(alphas, axis=1)


def _attend_tiles(carry, q_ref, tiles, s_ref, p_ref):
    ms, acc = carry
    q_all = q_ref[0].reshape(len(ms) * Q_BLOCK, HEAD_DIM)
    rows = [t[0].shape[0] for t in tiles]
    s_ref[0, :rows[0]] = _dot_nt(tiles[0][0], q_all)
    alpha_prev = None
    for i, (_, _, add, table) in enumerate(tiles):
        slot = i % 2
        if i + 1 < len(tiles):
            s_ref[1 - slot, :rows[i + 1]] = _dot_nt(tiles[i + 1][0], q_all)
        ms, alpha = _softmax_tile(slot, rows[i], ms, add, table, s_ref, p_ref)
        if i > 0:
            acc = acc * alpha_prev + _dot(tiles[i - 1][1], p_ref[1 - slot, :rows[i - 1]])
        alpha_prev = alpha
    last = len(tiles) - 1
    return ms, acc * alpha_prev + _dot(tiles[last][1], p_ref[last % 2, :rows[last]])


def _attend_init(heads, rows=V_ROWS):
    return ([jnp.full((1, Q_BLOCK), NEG, F32) for _ in range(heads)],
            jnp.zeros((rows, heads * Q_BLOCK), F32))


def _attend_finish(carry):
    _, acc = carry
    return acc[:HEAD_DIM] / acc[HEAD_DIM:HEAD_DIM + 1]


def _far_attend(qi, heads, n_tiles, q_ref, load_k, load_vt, load_add, s_ref, p_ref):
    T, C = Q_BLOCK, KEY_CHUNK
    pairs = (qi // 2 + 1) // 2
    q_all = q_ref[0].reshape(heads * T, HEAD_DIM)
    tile = lambda j: pl.multiple_of(jnp.clip(j, 0, n_tiles - 1) * C, C)
    logits = lambda j: _dot_nt(load_k(tile(j)), q_all)
    s_ref[0] = logits(0)
    p_ref[1] = jnp.zeros(p_ref.shape[1:], p_ref.dtype)

    softmax = lambda slot, j, ms: _softmax_tile(slot, C, ms, load_add(tile(j)), None, s_ref, p_ref)

    def body(i, carry):
        ms, acc, alpha_prev = carry
        a = 2 * i
        s_ref[1] = logits(a + 1)
        ms, alpha_a = softmax(0, a, ms)
        acc = acc * alpha_prev + _dot(load_vt(tile(a - 1)), p_ref[1])
        s_ref[0] = logits(a + 2)
        ms, alpha_b = softmax(1, a + 1, ms)
        acc = acc * alpha_a + _dot(load_vt(tile(a)), p_ref[0])
        return ms, acc, alpha_b

    ms, acc = _attend_init(heads)
    ms, acc, alpha_prev = lax.fori_loop(0, pairs, body, (ms, acc, jnp.ones((1, heads * T), F32)))
    acc = acc * alpha_prev + _dot(load_vt(tile(2 * pairs - 1)), p_ref[1])
    return ms, acc


def _nsa_kernel(q_ref, kc_ref, vcx_ref, ks_ref, vst_ref, kw_ref, vwt_ref, gate_ref, band_ref, tdiag_ref,
                o_ref, sc_ref, sel_ref, s_ref, p_ref):
    qi = pl.program_id(2)
    hg, T, C = NSA_GROUP, Q_BLOCK, KEY_CHUNK
    t0 = qi * T
    n_cp = kc_ref.shape[2]
    n_s = sel_ref.shape[1]
    row_c = lax.broadcasted_iota(I32, (C, T), 0)
    tpos_c = t0 + lax.broadcasted_iota(I32, (C, T), 1)
    diag_start = pl.multiple_of(jnp.maximum(t0 - T, 0), T)

    band0 = pl.multiple_of(jnp.maximum(t0 // CMP_STRIDE - 2 * SUBLANES, 0), SUBLANES)
    sc_ref[...] = jnp.where(lax.broadcasted_iota(I32, sc_ref.shape, 0) >= band0 + CMP_BAND, NEG, 0.0)
    sc_ref[pl.ds(band0, CMP_BAND), :] = band_ref[0]
    rows_c = min(C, n_cp)
    cmp_tiles = [(kc_ref[0, 0, lo:lo + rows_c, :], vcx_ref[0, 0, :, lo:lo + rows_c], None,
                  sc_ref[lo:lo + rows_c, :]) for lo in range(0, n_cp, rows_c)]
    m_c, acc_c = _attend_tiles(_attend_init(hg, vcx_ref.shape[2]), q_ref, cmp_tiles, s_ref, p_ref)
    seen = jnp.concatenate(m_c, axis=1) > 0.5 * NEG
    acc_c = acc_c * jnp.where(seen, 1.0 / acc_c[HEAD_DIM:HEAD_DIM + 1], 0.0)
    o_c = [acc_c[:HEAD_DIM, h * T:(h + 1) * T] for h in range(hg)]
    imp = acc_c[V_ROWS:, 0:T]
    for h in range(1, hg):
        imp = imp + acc_c[V_ROWS:, h * T:(h + 1) * T]

    blk = lax.broadcasted_iota(I32, (n_s, T), 0)
    blk_f = blk.astype(F32)
    cur = (t0 + lax.broadcasted_iota(I32, (n_s, T), 1)) // SLC_LEN
    forced = (blk == 0) | (blk == cur) | (blk == cur - 1)
    work = jnp.where(forced, -3e38, jnp.where(blk <= cur, imp, -1e6))

    def pick(_, carry):
        work, sel = carry
        mx = jnp.max(work, axis=0, keepdims=True)
        first = jnp.min(jnp.where(work == mx, blk_f, float(n_s)), axis=0, keepdims=True)
        hit = blk_f == first
        return jnp.where(hit, -3e38, work), jnp.where(hit, 1.0, sel)

    _, sel = lax.fori_loop(0, max(min(SLC_TOPN, n_s) - N_FORCED, 0), pick,
                           (work, jnp.where(forced, 1.0, 0.0)))
    sel_add = jnp.where(sel > 0.5, 0.0, NEG)
    sel_ref[0] = sel_add
    sel_ref[1] = jnp.where(blk < (t0 - T) // SLC_LEN, sel_add, NEG)

    def block_mask(start, which):
        first = start // SLC_LEN
        return jnp.concatenate([jnp.broadcast_to(sel_ref[which, pl.ds(first + i, 1), :], (SLC_LEN, T))
                                for i in range(C // SLC_LEN)], axis=0)

    load_ks = lambda start: ks_ref[0, 0, pl.ds(start, C), :]
    load_vst = lambda start: vst_ref[0, 0, :, pl.ds(start, C)]
    carry = _far_attend(qi, hg, ks_ref.shape[2] // C, q_ref, load_ks, load_vst,
                        lambda start: block_mask(start, 1), s_ref, p_ref)
    last_tile = (load_ks(diag_start), load_vst(diag_start), block_mask(diag_start, 0), tdiag_ref[0])
    o_s = _attend_finish(_attend_tiles(carry, q_ref, [last_tile], s_ref, p_ref))

    def win_tile(lo, last):
        start = pl.multiple_of(jnp.maximum(lo, 0), T)
        pos = row_c + start
        rel = tpos_c - pos
        add = jnp.where((pos < lo + C) & (rel >= 0) & (rel < WINDOW), 0.0, NEG)
        return (kw_ref[0, 0, pl.ds(start, C), :], vwt_ref[0, 0, :, pl.ds(start, C)], add,
                tdiag_ref[0] if last else None)

    win_tiles = [win_tile(t0 - T - 2 * C, False), win_tile(t0 - T - C, False), win_tile(t0 - T, True)]
    o_w = _attend_finish(_attend_tiles(_attend_init(hg), q_ref, win_tiles, s_ref, p_ref))

    gates = gate_ref[0, 0]
    for h in range(hg):
        cols = slice(h * T, (h + 1) * T)
        o = (gates[3 * h:3 * h + 1] * o_c[h] + gates[3 * h + 1:3 * h + 2] * o_s[:, cols]
             + gates[3 * h + 2:3 * h + 3] * o_w[:, cols])
        o_ref[0, :, h * HEAD_DIM:(h + 1) * HEAD_DIM] = o.T.astype(o_ref.dtype)


def _nsa(q, kc, vcx, ks, vst, kw, vwt, gates_t, band, tdiag):
    b, _, s, _ = q.shape
    g, hg = NSA_KV_HEADS, NSA_GROUP
    nq = s // Q_BLOCK
    n_cp = kc.shape[2]
    assert n_cp % min(KEY_CHUNK, n_cp) == 0
    rows = lambda r: pl.BlockSpec((1, 1, r, HEAD_DIM), lambda bi, gi, qi: (bi, gi, 0, 0))
    cols = lambda r, c: pl.BlockSpec((1, 1, r, c), lambda bi, gi, qi: (bi, gi, 0, 0))
    return pl.pallas_call(
        _nsa_kernel,
        grid=(b, g, nq),
        in_specs=[
            pl.BlockSpec((1, hg, Q_BLOCK, HEAD_DIM), lambda bi, gi, qi: (bi, gi, qi, 0)),
            rows(n_cp), cols(vcx.shape[2], n_cp), rows(s), cols(V_ROWS, s), rows(s), cols(V_ROWS, s),
            pl.BlockSpec((1, 1, hg * 3, Q_BLOCK), lambda bi, gi, qi: (bi, gi, 0, qi)),
            pl.BlockSpec((1, CMP_BAND, hg * Q_BLOCK), lambda bi, gi, qi: (jnp.minimum(qi, 2), 0, gi)),
            pl.BlockSpec((1, KEY_CHUNK, hg * Q_BLOCK), lambda bi, gi, qi: (jnp.minimum(qi, 1), 0, gi)),
        ],
        out_specs=pl.BlockSpec((1, Q_BLOCK, hg * HEAD_DIM), lambda bi, gi, qi: (bi, qi, gi)),
        out_shape=jax.ShapeDtypeStruct((b, s, NSA_HEADS * HEAD_DIM), BF16),
        scratch_shapes=[pltpu.VMEM((n_cp, hg * Q_BLOCK), F32), pltpu.VMEM((2, s // SLC_LEN, Q_BLOCK), F32),
                        pltpu.VMEM((2, KEY_CHUNK, hg * Q_BLOCK), F32), pltpu.VMEM((2, KEY_CHUNK, hg * Q_BLOCK), BF16)],
        compiler_params=_params("parallel", "parallel", "arbitrary"),
        name="nsa",
    )(q, kc, vcx, ks, vst, kw, vwt, gates_t, band, tdiag)


DIGIT_BITS = 14
DIGIT_BIAS = 0x80
DIGIT_DTYPE = jnp.bfloat16


def _pack_patterns(p):
    half = p.shape[0] // 2
    return (p[:half] << 16) | p[half:]


def _sortable_key(x):
    b = lax.bitcast_convert_type(jnp.where(x == 0.0, 0.0, x), I32)
    return b ^ ((b >> 31) & 0x7FFFFFFF)


def _dsa_kernel(q_ref, iq_ref, iw_ref, ik_ref, k_ref, vt_ref, tdiag_ref, o_ref, key_ref, dig_ref, s_ref, p_ref,
                *, n_keep):
    qi = pl.program_id(1)
    H, T, C = DSA_HEADS, Q_BLOCK, KEY_CHUNK
    t0 = qi * T
    n_chunks = qi // 2 + 1
    iw = iw_ref[0]
    row = lax.broadcasted_iota(I32, (C, T), 0)
    tpos = t0 + lax.broadcasted_iota(I32, (C, T), 1)
    int_min = jnp.iinfo(jnp.int32).min

    def score_step(j, _):
        for u in range(2):
            start = pl.multiple_of((2 * j + u) * C, C)
            ikc = ik_ref[0, pl.ds(start, C), :]
            score = jnp.maximum(_dot_nt(ikc, iq_ref[0, 0]), 0.0) * iw[0:1]
            for h in range(1, IDX_HEADS):
                score = score + jnp.maximum(_dot_nt(ikc, iq_ref[0, h]), 0.0) * iw[h:h + 1]
            score = jnp.where(row + start <= tpos, score, NEG)
            key = _sortable_key(score)
            key_ref[pl.ds(start, C), :] = key
            digit = lax.shift_right_logical(key ^ int_min, jnp.int32(32 - DIGIT_BITS))
            dig_ref[pl.ds(pl.multiple_of(start // 2, C // 2), C // 2), :] = _pack_patterns(digit + DIGIT_BIAS)
        return 0

    lax.fori_loop(0, (n_chunks + 1) // 2, score_step, 0)

    def count_digits(cand_digit):
        cand = cand_digit + DIGIT_BIAS
        cand = pltpu.bitcast(jnp.broadcast_to((cand << 16) | cand, (SUBLANES, T)), DIGIT_DTYPE)

        def tile_hits(j):
            words = dig_ref[pl.ds(pl.multiple_of(j * (C // 2), C // 2), C // 2), :]
            vals = pltpu.bitcast(words, DIGIT_DTYPE).reshape(C // (2 * SUBLANES), 2 * SUBLANES, T)
            hits = jnp.where(vals >= cand[None], jnp.ones((), DIGIT_DTYPE), jnp.zeros((), DIGIT_DTYPE))
            parts = [hits[i] for i in range(hits.shape[0])]
            while len(parts) > 1:
                parts = [parts[i] + parts[i + 1] for i in range(0, len(parts), 2)]
            return parts[0].astype(F32)

        def body(i, acc):
            second = 2 * i + 1
            keep = jnp.where(second < n_chunks, 1.0, 0.0)
            return acc + tile_hits(2 * i) + tile_hits(jnp.minimum(second, n_chunks - 1)) * keep

        acc = lax.fori_loop(0, (n_chunks + 1) // 2, body, jnp.zeros((2 * SUBLANES, T), F32))
        return jnp.sum(acc, axis=0, keepdims=True).astype(I32)

    def digit_search(need, n_all):
        def bit(i, carry):
            digit, n_at = carry
            cand = digit | lax.shift_left(jnp.int32(1), DIGIT_BITS - 1 - i)
            n_ge = count_digits(cand)
            ok = n_ge >= need
            return jnp.where(ok, cand, digit), jnp.where(ok, n_ge, n_at)

        return lax.fori_loop(0, DIGIT_BITS, bit, (jnp.zeros((1, T), I32), n_all))

    def count(pred_fn):
        def body(j, acc):
            start = pl.multiple_of(j * C, C)
            hits = jnp.where(pred_fn(key_ref[pl.ds(start, C), :], start), 1.0, 0.0)
            return acc + jnp.sum(hits.reshape(C // COUNT_ROWS, COUNT_ROWS, T), axis=0)
        acc = lax.fori_loop(0, n_chunks, body, jnp.zeros((COUNT_ROWS, T), F32))
        return jnp.sum(acc, axis=0, keepdims=True).astype(I32)

    d1, n_at1 = digit_search(n_keep, jnp.full((1, T), 1, I32) * (n_chunks * C))
    n_gt1 = count_digits(d1 + 1)

    def second_digits(j, _):
        start = pl.multiple_of(j * C, C)
        u = key_ref[pl.ds(start, C), :] ^ int_min
        digit = lax.shift_right_logical(u, jnp.int32(32 - 2 * DIGIT_BITS)) & (2 ** DIGIT_BITS - 1)
        first = lax.shift_right_logical(u, jnp.int32(32 - DIGIT_BITS))
        pattern = jnp.where(first == d1, digit + DIGIT_BIAS, 0)
        dig_ref[pl.ds(pl.multiple_of(start // 2, C // 2), C // 2), :] = _pack_patterns(pattern)
        return 0

    lax.fori_loop(0, n_chunks, second_digits, 0)
    d2, n_at2 = digit_search(n_keep - n_gt1, n_at1 - n_gt1)

    def thr_bit(i, carry):
        thr_u, n_at = carry
        cand_u = thr_u | lax.shift_left(jnp.int32(1), 31 - i)
        cand = cand_u ^ int_min
        n_ge = count(lambda keys, _: keys >= cand)
        ok = n_ge >= n_keep
        return jnp.where(ok, cand_u, thr_u), jnp.where(ok, n_ge, n_at)

    prefix = lax.shift_left(d1, jnp.int32(32 - DIGIT_BITS)) | lax.shift_left(d2, jnp.int32(32 - 2 * DIGIT_BITS))
    thr_u, n_at = lax.fori_loop(2 * DIGIT_BITS, 32, thr_bit, (prefix, n_gt1 + n_at2))
    thr = thr_u ^ int_min

    @pl.when(jnp.max((n_at - n_keep).astype(F32)) > 0.0)
    def _():
        need = n_keep - count(lambda keys, _: keys > thr)
        n_pos_bits = int(math.ceil(math.log2(key_ref.shape[0] + 1)))

        def pos_bit(i, lim):
            cand = lim + lax.shift_left(jnp.int32(1), n_pos_bits - 1 - i)
            n_tied = count(lambda keys, start: (keys == thr) & (row + start < cand))
            return jnp.where(n_tied <= need, cand, lim)

        lim = lax.fori_loop(0, n_pos_bits, pos_bit, jnp.zeros((1, T), I32))

        def drop(j, _):
            start = pl.multiple_of(j * C, C)
            keys = key_ref[pl.ds(start, C), :]
            key_ref[pl.ds(start, C), :] = jnp.where((keys == thr) & (row + start >= lim), int_min, keys)
            return 0

        lax.fori_loop(0, n_chunks, drop, 0)

    def key_mask(start, far):
        keep = key_ref[pl.ds(start, C), :] >= thr
        if far:
            keep = keep & (row + start < t0 - T)
        return jnp.where(keep, 0.0, NEG)

    load_k = lambda start: k_ref[0, pl.ds(start, C), :]
    load_vt = lambda start: vt_ref[0, :, pl.ds(start, C)]
    carry = _far_attend(qi, H, k_ref.shape[1] // C, q_ref, load_k, load_vt, lambda start: key_mask(start, True), s_ref, p_ref)
    diag_start = pl.multiple_of(jnp.maximum(t0 - T, 0), T)
    last_tile = (load_k(diag_start), load_vt(diag_start), key_mask(diag_start, False), tdiag_ref[0])
    o = _attend_finish(_attend_tiles(carry, q_ref, [last_tile], s_ref, p_ref))
    for h in range(H):
        o_ref[0, :, h * HEAD_DIM:(h + 1) * HEAD_DIM] = o[:, h * T:(h + 1) * T].T.astype(o_ref.dtype)


def _dsa(q, iq, iw_t, ik, k, vt, tdiag):
    b, _, s, _ = q.shape
    assert s % (2 * KEY_CHUNK) == 0
    nq = s // Q_BLOCK
    n_keep = min(DSA_TOPK_MAX, s // 4)
    heads = lambda: pl.BlockSpec((1, DSA_HEADS, Q_BLOCK, HEAD_DIM), lambda bi, qi: (bi, 0, qi, 0))
    rows = lambda: pl.BlockSpec((1, s, HEAD_DIM), lambda bi, qi: (bi, 0, 0))
    return pl.pallas_call(
        functools.partial(_dsa_kernel, n_keep=n_keep),
        grid=(b, nq),
        in_specs=[heads(), heads(),
                  pl.BlockSpec((1, IDX_HEADS, Q_BLOCK), lambda bi, qi: (bi, 0, qi)),
                  rows(), rows(),
                  pl.BlockSpec((1, V_ROWS, s), lambda bi, qi: (bi, 0, 0)),
                  pl.BlockSpec((1, KEY_CHUNK, DSA_HEADS * Q_BLOCK), lambda bi, qi: (jnp.minimum(qi, 1), 0, 1))],
        out_specs=pl.BlockSpec((1, Q_BLOCK, DSA_HEADS * HEAD_DIM), lambda bi, qi: (bi, qi, 0)),
        out_shape=jax.ShapeDtypeStruct((b, s, DSA_HEADS * HEAD_DIM), BF16),
        scratch_shapes=[pltpu.VMEM((s, Q_BLOCK), I32), pltpu.VMEM((s // 2, Q_BLOCK), I32),
                        pltpu.VMEM((2, KEY_CHUNK, DSA_HEADS * Q_BLOCK), F32),
                        pltpu.VMEM((2, KEY_CHUNK, DSA_HEADS * Q_BLOCK), BF16)],
        compiler_params=_params("parallel", "arbitrary"),
        name="dsa",
    )(q, iq, iw_t, ik, k, vt, tdiag)


def _merge_kernel(x_ref, oa_ref, ob_ref, ga_ref, gb_ref, wa_ref, wb_ref, wo_ref, gn_ref, wr_ref, br_ref,
                  x1_ref, xn_ref, gate_ref, gate_t_ref):
    merged = ga_ref[...] * _dot(oa_ref[0], wa_ref[...]) + gb_ref[...] * _dot(ob_ref[0], wb_ref[...])
    x1 = x_ref[...] + _dot(merged.astype(BF16), wo_ref[...])
    x1_ref[...] = x1
    xn = _row_rms(x1, gn_ref[...]).astype(BF16)
    xn_ref[...] = xn

    logits = _dot(xn, wr_ref[...]) + br_ref[...]
    lane = lax.broadcasted_iota(I32, logits.shape, 1)
    work = jnp.where(lane < N_EXPERTS, logits, -3e38)
    gates = jnp.zeros_like(logits)
    denom = jnp.zeros((logits.shape[0], 1), F32)
    top = None
    for _ in range(MOE_TOPK):
        mx = jnp.max(work, axis=1, keepdims=True)
        first = jnp.min(jnp.where(work == mx, lane, LANES), axis=1, keepdims=True)
        hit = lane == first
        top = mx if top is None else top
        e = jnp.exp(mx - top)
        gates = jnp.where(hit, e, gates)
        denom = denom + e
        work = jnp.where(hit, -3e38, work)
    gates = gates / denom
    gate_ref[...] = gates
    gate_t_ref[...] = gates.T


def _merge(x, oa, ob, ga, gb, w_branch_a, w_branch_b, w_out, ffn_norm, w_router, b_router, tm):
    b, s, d = x.shape
    n = b * s
    xf = x.reshape(n, d)
    nt = s // tm
    wa = w_branch_a.astype(BF16)
    wb = w_branch_b.astype(BF16)
    wr = jnp.concatenate([w_router, jnp.zeros((d, LANES - N_EXPERTS), F32)], axis=1).astype(BF16)
    br = jnp.concatenate([b_router, jnp.zeros((LANES - N_EXPERTS,), F32)]).reshape(1, LANES)
    consts = [wa, wb, w_out.astype(BF16), ffn_norm.reshape(1, d), wr, br]
    row = lambda w: pl.BlockSpec((tm, w), lambda bi, ti: (bi * nt + ti, 0))
    heads = lambda: pl.BlockSpec((1, tm, NSA_HEADS * HEAD_DIM), lambda bi, ti: (bi, ti, 0))
    return pl.pallas_call(
        _merge_kernel,
        grid=(b, nt),
        in_specs=[row(d), heads(), heads(), row(d), row(d)] + [_const_spec(c.shape) for c in consts],
        out_specs=[row(d), row(d), row(LANES), pl.BlockSpec((LANES, tm), lambda bi, ti: (0, bi * nt + ti))],
        out_shape=[jax.ShapeDtypeStruct((n, d), F32), jax.ShapeDtypeStruct((n, d), BF16),
                   jax.ShapeDtypeStruct((n, LANES), F32), jax.ShapeDtypeStruct((LANES, n), F32)],
        compiler_params=_params("parallel", "parallel"),
        name="merge",
    )(xf, oa, ob, ga, gb, *consts)


def _deinterleave_index(width):
    half = MXU_DIM // 2
    c = np.arange(width)
    r = c % MXU_DIM
    return (c - r) + np.where(r < half, 2 * r, 2 * (r - half) + 1)


def _moe_prep_kernel(w_ref, perm_ref, o_ref):
    perm = perm_ref[...]
    for g in range(w_ref.shape[2] // MXU_DIM):
        cols = slice(g * MXU_DIM, (g + 1) * MXU_DIM)
        o_ref[0, :, cols] = _dot(w_ref[0, :, cols].astype(BF16), perm).astype(BF16)


def _moe_prep(w_gate_up, tr):
    e, d, f2 = w_gate_up.shape
    src = _deinterleave_index(MXU_DIM)
    perm = jnp.asarray((np.arange(MXU_DIM)[:, None] == src[None, :]).astype(np.float32), BF16)
    return pl.pallas_call(
        _moe_prep_kernel,
        grid=(e, d // tr),
        in_specs=[pl.BlockSpec((1, tr, f2), lambda i, j: (i, j, 0)), _const_spec(perm.shape)],
        out_specs=pl.BlockSpec((1, tr, f2), lambda i, j: (i, j, 0)),
        out_shape=jax.ShapeDtypeStruct((e, d, f2), BF16),
        compiler_params=_params("parallel", "parallel"),
        name="moe_prep",
    )(w_gate_up, perm)


def _moe_kernel(x1_ref, xn_ref, gate_ref, gate_t_ref, lower_ref, upper_ref, wgu_ref, bgu_ref, wd_ref, bd_ref,
                o_ref, rank_ref, rank_t_ref):
    e = pl.program_id(1)
    tm, R = xn_ref.shape[0], MOE_ROWS

    @pl.when(e == 0)
    def _():
        o_ref[...] = x1_ref[...]
        routed = jnp.where(gate_ref[...] > 0.0, 1.0, 0.0).astype(BF16)
        routed_t = jnp.where(gate_t_ref[...] > 0.0, 1.0, 0.0).astype(BF16)
        rank_ref[...] = _dot(lower_ref[...], routed)
        rank_t_ref[...] = _dot(routed_t, upper_ref[...])

    g_row = gate_t_ref[pl.ds(e, 1), :]
    is_e = lax.broadcasted_iota(I32, (tm, LANES), 1) == e
    g_col = jnp.sum(jnp.where(is_e, gate_ref[...], 0.0), axis=1, keepdims=True)
    rank_row = jnp.where(g_row > 0.0, rank_t_ref[pl.ds(e, 1), :], -1.0)
    rank_col = jnp.where(g_col > 0.0, jnp.sum(jnp.where(is_e, rank_ref[...], 0.0), axis=1, keepdims=True), -1.0)
    n_routed = jnp.sum(jnp.where(g_row > 0.0, 1.0, 0.0)).astype(I32)
    slot_r = lax.broadcasted_iota(I32, (R, tm), 0).astype(F32)
    slot_c = lax.broadcasted_iota(I32, (tm, R), 1).astype(F32)
    half = MXU_DIM // 2

    def block(j, _):
        base = (j * R).astype(F32)
        pick = jnp.where(rank_row - base == slot_r, 1.0, 0.0)
        xg = _dot(pick.astype(BF16), xn_ref[...]).astype(BF16)
        hu = _dot(xg, wgu_ref[0]) + bgu_ref[0]
        acts = []
        for g in range(hu.shape[1] // MXU_DIM):
            glu = jnp.minimum(hu[:, g * MXU_DIM:g * MXU_DIM + half], SWIGLU_LIMIT)
            lin = jnp.clip(hu[:, g * MXU_DIM + half:(g + 1) * MXU_DIM], -SWIGLU_LIMIT, SWIGLU_LIMIT)
            acts.append((glu * jax.nn.sigmoid(SWIGLU_ALPHA * glu) * (lin + 1.0)).astype(BF16))
        y = _dot(jnp.concatenate(acts, axis=1), wd_ref[0]) + bd_ref[0]
        g_r = jnp.sum(pick * g_row, axis=1, keepdims=True)
        put = jnp.where(rank_col - base == slot_c, 1.0, 0.0).astype(BF16)
        o_ref[...] += _dot(put, (y * g_r).astype(BF16))
        return 0

    lax.fori_loop(0, (n_routed + R - 1) // R, block, 0)


def _moe(x1, xn, gates, gates_t, w_gate_up, b_gate_up, w_down, b_down, tm):
    n, d = x1.shape
    f = D_EXPERT
    wgu = _moe_prep(w_gate_up, tr=256)
    bgu = b_gate_up[:, _deinterleave_index(2 * f)].reshape(N_EXPERTS, 1, 2 * f)
    wd = w_down.astype(BF16)
    bd = b_down.reshape(N_EXPERTS, 1, d)
    earlier = np.arange(tm)[None, :] < np.arange(tm)[:, None]
    lower = jnp.asarray(earlier.astype(np.float32), BF16)
    upper = jnp.asarray(earlier.T.astype(np.float32), BF16)
    row = lambda w: pl.BlockSpec((tm, w), lambda i, e: (i, 0))
    per_e = lambda a, b_: pl.BlockSpec((1, a, b_), lambda i, e: (e, 0, 0))
    return pl.pallas_call(
        _moe_kernel,
        grid=(n // tm, N_EXPERTS),
        in_specs=[row(d), row(d), row(LANES), pl.BlockSpec((LANES, tm), lambda i, e: (0, i)),
                  _const_spec(lower.shape), _const_spec(upper.shape),
                  per_e(d, 2 * f), per_e(1, 2 * f), per_e(f, d), per_e(1, d)],
        out_specs=row(d),
        out_shape=jax.ShapeDtypeStruct((n, d), F32),
        scratch_shapes=[pltpu.VMEM((tm, LANES), F32), pltpu.VMEM((LANES, tm), F32)],
        compiler_params=_params("parallel", "arbitrary"),
        name="moe",
    )(x1, xn, gates, gates_t, lower, upper, wgu, bgu, wd, bd)


def _rel_bucket(rel):
    n = jnp.maximum(rel, 0)
    exact = REL_BUCKETS // 2
    nf = jnp.maximum(n, 1).astype(F32)
    large = exact + (jnp.log(nf / exact) / math.log(REL_MAX_DIST / exact) * (REL_BUCKETS - exact)).astype(I32)
    large = jnp.minimum(large, REL_BUCKETS - 1)
    return jnp.where(n < exact, n, large)


def _rel_table(rel_bias, rel):
    far = rel_bias[REL_BUCKETS - 1]
    tab = jnp.where(rel[..., None] >= 0, (rel_bias[_rel_bucket(rel)] - far) * LOG2E, NEG)
    tab = jnp.moveaxis(tab, -1, -2)
    return tab.reshape(tab.shape[:-2] + (-1,)).astype(F32)


def _bias_tables(rel_bias):
    i = np.arange(Q_BLOCK)[None, :]
    j = np.arange(KEY_CHUNK)[:, None]
    tdiag = _rel_table(rel_bias, jnp.asarray(np.stack([i - j, i - j + Q_BLOCK])))
    r = np.arange(CMP_BAND)[:, None]
    rel_band = np.stack([i - (CMP_LEN - 1) - CMP_STRIDE * (c0 + r) for c0 in (0, -SUBLANES, -2 * SUBLANES)])
    band = _rel_table(rel_bias[:, :NSA_HEADS], jnp.asarray(rel_band))
    return tdiag, band


def _overlap_matrix_t(s):
    n_cp = s // CMP_STRIDE
    n_s = s // SLC_LEN
    c_start = np.arange(n_cp) * CMP_STRIDE
    s_start = np.arange(n_s) * SLC_LEN
    ovl = (c_start[None, :] < s_start[:, None] + SLC_LEN) & (c_start[None, :] + CMP_LEN > s_start[:, None])
    ovl[:, n_cp - 1] = False
    return jnp.asarray(ovl.astype(np.float32), BF16)


def _layer(x, attn_norm, w_in, nsa_q_norm, nsa_k_norm, cmp_k_pe, cmp_k_w1, cmp_k_b1, cmp_k_w2,
           cmp_v_pe, cmp_v_w1, cmp_v_b1, cmp_v_w2, dsa_kv_norm, dsa_w_uk, dsa_w_uv, dsa_q_norm,
           dsa_k_norm, w_branch_a, w_branch_b, w_out, rel_bias, ffn_norm, w_router, b_router,
           w_gate_up, b_gate_up, w_down, b_down):
    b, s, d = x.shape
    n = b * s
    g = NSA_KV_HEADS
    (qa, ck, cv, ks, sv_t, kw, wv_t, misc_t, qb, kb, vb_t, iq, ik, ga, gb) = _in_proj(
        x, attn_norm, w_in, nsa_q_norm, nsa_k_norm, dsa_q_norm, dsa_kv_norm, dsa_w_uk, dsa_w_uv, dsa_k_norm,
        tm=min(256, s))

    chunks = lambda a: a.reshape(b, s // CMP_STRIDE, CMP_STRIDE, g, HEAD_DIM).transpose(0, 3, 1, 2, 4).reshape(
        b, g, s // CMP_STRIDE, CMP_STRIDE * HEAD_DIM)
    kc = _compress(chunks(ck), cmp_k_pe, cmp_k_w1, cmp_k_b1, cmp_k_w2, nsa_k_norm, True)
    vc = _compress(chunks(cv), cmp_v_pe, cmp_v_w1, cmp_v_b1, cmp_v_w2, nsa_k_norm, False)

    tdiag, band = _bias_tables(rel_bias)
    gates_t = misc_t[:, :NSA_HEADS * 3].reshape(b, g, NSA_GROUP * 3, s)
    overlap_t = _overlap_matrix_t(s)
    vc_t = vc.transpose(0, 1, 3, 2)
    vcx = jnp.concatenate([vc_t, jnp.ones((b, g, V_ROWS - HEAD_DIM, vc_t.shape[3]), BF16),
                           jnp.broadcast_to(overlap_t, (b, g) + overlap_t.shape)], axis=2)
    oa = _nsa(qa, kc, vcx, ks, sv_t, kw, wv_t, gates_t, band, tdiag)

    iw_t = misc_t[:, NSA_HEADS * 3:NSA_HEADS * 3 + IDX_HEADS]
    ob = _dsa(qb, iq, iw_t, ik.reshape(b, s, IDX_DIM), kb.reshape(b, s, HEAD_DIM), vb_t, tdiag)

    x1, xn, moe_gates, moe_gates_t = _merge(x, oa, ob, ga, gb, w_branch_a, w_branch_b, w_out, ffn_norm, w_router,
                               b_router, tm=min(512, s))
    out = _moe(x1, xn, moe_gates, moe_gates_t, w_gate_up, b_gate_up, w_down, b_down, tm=min(MOE_TOKENS, n))
    return out.reshape(b, s, d)


def kernel(x, attn_norm, w_in, nsa_q_norm, nsa_k_norm, cmp_k_pe, cmp_k_w1, cmp_k_b1, cmp_k_w2, cmp_v_pe, cmp_v_w1, cmp_v_b1, cmp_v_w2, dsa_kv_norm, dsa_w_uk, dsa_w_uv, dsa_q_norm, dsa_k_norm, w_branch_a, w_branch_b, w_out, rel_bias, ffn_norm, w_router, b_router, w_gate_up, b_gate_up, w_down, b_down):
    for l in range(attn_norm.shape[0]):
        x = _layer(x, attn_norm[l], w_in[l], nsa_q_norm[l], nsa_k_norm[l], cmp_k_pe[l], cmp_k_w1[l],
                   cmp_k_b1[l], cmp_k_w2[l], cmp_v_pe[l], cmp_v_w1[l], cmp_v_b1[l], cmp_v_w2[l],
                   dsa_kv_norm[l], dsa_w_uk[l], dsa_w_uv[l], dsa_q_norm[l], dsa_k_norm[l],
                   w_branch_a[l], w_branch_b[l], w_out[l], rel_bias, ffn_norm[l], w_router[l],
                   b_router[l], w_gate_up[l], b_gate_up[l], w_down[l], b_down[l])
    return x
```

```python
import functools
import math

import numpy as np
import jax
import jax.numpy as jnp
from jax import lax
from jax.experimental import pallas as pl
from jax.experimental.pallas import tpu as pltpu

D_MODEL = 1024
HEAD_DIM = 64
NSA_HEADS = 8
NSA_KV_HEADS = 2
NSA_GROUP = NSA_HEADS // NSA_KV_HEADS
CMP_LEN = 32
CMP_STRIDE = 16
CMP_HID = 128
SLC_LEN = 64
SLC_TOPN = 16
WINDOW = 512
DSA_HEADS = 8
DSA_KV_RANK = 128
IDX_HEADS = 8
IDX_DIM = 64
DSA_TOPK_MAX = 256
N_EXPERTS = 32
MOE_TOPK = 4
D_EXPERT = 1024
SWIGLU_LIMIT = 7.0
SWIGLU_ALPHA = 1.702
REL_BUCKETS = 32
REL_MAX_DIST = 128
NORM_EPS = 1e-6
NEG = -1e30
LOG2E = math.log2(math.e)

Q_BLOCK = 128
KEY_CHUNK = 256
N_FORCED = 3
CMP_BAND = 24
COUNT_ROWS = 32
LANES = 128
SUBLANES = 8
MXU_DIM = 256
V_ROWS = HEAD_DIM + SUBLANES
MOE_TOKENS = 1024
MOE_ROWS = 160
VMEM_LIMIT = 56 * 1024 * 1024

F32 = jnp.float32
BF16 = jnp.bfloat16
I32 = jnp.int32


def _params(*sem):
    return pltpu.CompilerParams(dimension_semantics=sem, vmem_limit_bytes=VMEM_LIMIT)


def _dot(a, b):
    return jnp.dot(a, b, preferred_element_type=F32)


def _dot_nt(a, b):
    return lax.dot_general(a, b, (((1,), (1,)), ((), ())), preferred_element_type=F32)


def _const_spec(shape):
    nd = len(shape)
    return pl.BlockSpec(shape, lambda *_: (0,) * nd)


def _segment_sumsq(y, seg_ones):
    y2 = y * y
    hi = y2.astype(BF16)
    lo = (y2 - hi.astype(F32)).astype(BF16)
    return _dot(hi, seg_ones) + _dot(lo, seg_ones)


def _head_rms(y, seg_ones, gain):
    ms = _segment_sumsq(y, seg_ones) * (1.0 / HEAD_DIM)
    return y * lax.rsqrt(ms + NORM_EPS) * gain


def _row_rms(y, gain):
    ms = jnp.mean(y * y, axis=-1, keepdims=True)
    return y * lax.rsqrt(ms + NORM_EPS) * gain


def _in_proj_kernel(x_ref, g_ref, wqa_ref, wkv_ref, wmisc_ref, wqb_ref, wckv_ref, wiq_ref, wik_ref,
                    wga_ref, wgb_ref, seg512_ref, seg128_ref, gqa_ref, gka_ref, gqb_ref, gckv_ref,
                    wuk_ref, wuv_ref, gkb_ref,
                    qa_ref, ck_ref, cv_ref, ks_ref, sv_ref, kw_ref, wv_ref, misc_ref, qb_ref,
                    kb_ref, vb_ref, iq_ref, ik_ref, ga_ref, gb_ref):
    h = _row_rms(x_ref[...], g_ref[...]).astype(BF16)
    scale = HEAD_DIM ** -0.5 * LOG2E
    seg512 = seg512_ref[...]
    seg128 = seg128_ref[...]

    def put_heads(o_ref, y):
        for i in range(o_ref.shape[1]):
            o_ref[0, i] = y[:, i * HEAD_DIM:(i + 1) * HEAD_DIM].astype(o_ref.dtype)

    def put_values_t(o_ref, y):
        y_t = y.T
        ones = jnp.ones((V_ROWS - HEAD_DIM, y.shape[0]), o_ref.dtype)
        for i in range(o_ref.shape[1]):
            o_ref[0, i, :HEAD_DIM, :] = y_t[i * HEAD_DIM:(i + 1) * HEAD_DIM].astype(o_ref.dtype)
            o_ref[0, i, HEAD_DIM:, :] = ones

    qa = _head_rms(_dot(h, wqa_ref[...]), seg512, gqa_ref[...])
    put_heads(qa_ref, qa * scale)

    kv = _dot(h, wkv_ref[...])
    ck_ref[...] = kv[:, 0:128]
    cv_ref[...] = kv[:, 128:256]
    put_heads(ks_ref, _head_rms(kv[:, 256:384], seg128, gka_ref[...]))
    put_values_t(sv_ref, kv[:, 384:512])
    put_heads(kw_ref, _head_rms(kv[:, 512:640], seg128, gka_ref[...]))
    put_values_t(wv_ref, kv[:, 640:768])

    misc = _dot(h, wmisc_ref[...])
    lane = lax.broadcasted_iota(I32, misc.shape, 1)
    iw_scale = (IDX_HEADS ** -0.5)
    idx_scale = IDX_DIM ** -0.5
    misc = jnp.where(lane < NSA_HEADS * 3, jax.nn.sigmoid(misc), (misc * iw_scale) * idx_scale)
    misc_ref[0] = misc.T[:misc_ref.shape[1]]

    qb = _head_rms(_dot(h, wqb_ref[...]), seg512, gqb_ref[...])
    put_heads(qb_ref, qb * scale)

    ckv = _row_rms(_dot(h, wckv_ref[...]), gckv_ref[...]).astype(BF16)
    kb_ref[...] = _row_rms(_dot(ckv, wuk_ref[...]), gkb_ref[...]).astype(BF16)
    vb = _dot(ckv, wuv_ref[...])
    vb_ref[0, :HEAD_DIM, :] = vb.T.astype(BF16)
    vb_ref[0, HEAD_DIM:, :] = jnp.ones((V_ROWS - HEAD_DIM, vb.shape[0]), BF16)

    put_heads(iq_ref, _dot(h, wiq_ref[...]))
    ik_ref[...] = _dot(h, wik_ref[...]).astype(BF16)
    ga_ref[...] = jax.nn.sigmoid(_dot(h, wga_ref[...]))
    gb_ref[...] = jax.nn.sigmoid(_dot(h, wgb_ref[...]))


def _in_proj(x, attn_norm, w_in, nsa_q_norm, nsa_k_norm, dsa_q_norm, dsa_kv_norm, dsa_w_uk, dsa_w_uv,
             dsa_k_norm, tm):
    b, s, _ = x.shape
    n = b * s
    xf = x.reshape(n, -1)
    kvw = NSA_KV_HEADS * HEAD_DIM
    widths = (NSA_HEADS * HEAD_DIM, kvw, kvw, kvw, kvw, kvw, kvw, NSA_HEADS * 3,
              DSA_HEADS * HEAD_DIM, DSA_KV_RANK, IDX_HEADS * IDX_DIM, IDX_DIM, IDX_HEADS, D_MODEL, D_MODEL)
    offs = np.concatenate([[0], np.cumsum(widths)])
    sec = [w_in[:, offs[i]:offs[i + 1]] for i in range(len(widths))]
    wb = lambda w: w.astype(BF16)
    wqa = wb(sec[0])
    wkv = wb(jnp.concatenate(sec[1:7], axis=1))
    wmisc = wb(jnp.concatenate([sec[7], sec[12], jnp.zeros((D_MODEL, LANES - 32), F32)], axis=1))
    wqb, wckv, wiq, wik, wga, wgb = wb(sec[8]), wb(sec[9]), wb(sec[10]), wb(sec[11]), wb(sec[13]), wb(sec[14])

    def seg_ones(w):
        i = np.arange(w) // HEAD_DIM
        return jnp.asarray((i[:, None] == i[None, :]).astype(np.float32), BF16)

    consts = [attn_norm.reshape(1, -1), wqa, wkv, wmisc, wqb, wckv, wiq, wik, wga, wgb,
              seg_ones(512), seg_ones(128),
              jnp.tile(nsa_q_norm, NSA_HEADS).reshape(1, -1), jnp.tile(nsa_k_norm, NSA_KV_HEADS).reshape(1, -1),
              jnp.tile(dsa_q_norm, DSA_HEADS).reshape(1, -1), dsa_kv_norm.reshape(1, -1),
              wb(dsa_w_uk), wb(dsa_w_uv), dsa_k_norm.reshape(1, -1)]
    nt = s // tm
    g = NSA_KV_HEADS
    rows = lambda w, dt: (jax.ShapeDtypeStruct((n, w), dt), pl.BlockSpec((tm, w), lambda bi, ti: (bi * nt + ti, 0)))
    heads = lambda hh: (jax.ShapeDtypeStruct((b, hh, s, HEAD_DIM), BF16),
                        pl.BlockSpec((1, hh, tm, HEAD_DIM), lambda bi, ti: (bi, 0, ti, 0)))
    values_t = (jax.ShapeDtypeStruct((b, g, V_ROWS, s), BF16),
                pl.BlockSpec((1, g, V_ROWS, tm), lambda bi, ti: (bi, 0, 0, ti)))
    cols = lambda r, dt: (jax.ShapeDtypeStruct((b, r, s), dt), pl.BlockSpec((1, r, tm), lambda bi, ti: (bi, 0, ti)))
    outs = [heads(NSA_HEADS), rows(128, F32), rows(128, F32), heads(g), values_t, heads(g), values_t,
            cols(32, F32), heads(DSA_HEADS), rows(HEAD_DIM, BF16), cols(V_ROWS, BF16), heads(IDX_HEADS),
            rows(IDX_DIM, BF16), rows(D_MODEL, F32), rows(D_MODEL, F32)]
    return pl.pallas_call(
        _in_proj_kernel,
        grid=(b, nt),
        in_specs=[rows(D_MODEL, F32)[1]] + [_const_spec(c.shape) for c in consts],
        out_specs=[spec for _, spec in outs],
        out_shape=[shape for shape, _ in outs],
        compiler_params=_params("parallel", "parallel"),
        name="in_proj",
    )(xf, *consts)


def _compress_kernel(c_ref, pea_ref, peb_ref, w1a_ref, w1b_ref, b1_ref, w2_ref, gain_ref, o_ref, *, norm, n_c):
    c = c_ref[0, 0]
    ua = _dot((c + pea_ref[...]).astype(BF16), w1a_ref[...])
    ub = _dot((c + peb_ref[...]).astype(BF16), w1b_ref[...])
    rows = c.shape[0]
    ub_next = pltpu.roll(ub, rows - 1, 0)
    hid = jax.nn.gelu(ua + ub_next + b1_ref[...])
    o = _dot(hid.astype(BF16), w2_ref[...])
    if norm:
        o = _row_rms(o, gain_ref[...])
    ridx = lax.broadcasted_iota(I32, o.shape, 0)
    o_ref[0, 0] = jnp.where(ridx < n_c, o, 0.0).astype(o_ref.dtype)


def _compress(c, pe, w1, b1, w2, gain, norm):
    b, g, rows, width = c.shape
    half = CMP_STRIDE * HEAD_DIM
    consts = [pe[:CMP_STRIDE].reshape(1, half), pe[CMP_STRIDE:].reshape(1, half),
              w1[:half].astype(BF16), w1[half:].astype(BF16), b1.reshape(1, -1), w2.astype(BF16),
              gain.reshape(1, -1)]
    return pl.pallas_call(
        functools.partial(_compress_kernel, norm=norm, n_c=rows - 1),
        grid=(b, g),
        in_specs=[pl.BlockSpec((1, 1, rows, width), lambda i, j: (i, j, 0, 0))] + [_const_spec(x.shape) for x in consts],
        out_specs=pl.BlockSpec((1, 1, rows, HEAD_DIM), lambda i, j: (i, j, 0, 0)),
        out_shape=jax.ShapeDtypeStruct((b, g, rows, HEAD_DIM), BF16),
        compiler_params=_params("parallel", "parallel"),
        name="compress_norm" if norm else "compress",
    )(c, *consts)


def _softmax_tile(slot, rows, ms, add, table, s_ref, p_ref):
    T = Q_BLOCK
    new_m, alphas = [], []
    for h in range(len(ms)):
        cols = slice(h * T, (h + 1) * T)
        s = s_ref[slot, :rows, cols]
        if add is not None:
            s = s + add
        if table is not None:
            s = s + table[:, cols]
        m_new = jnp.maximum(ms[h], jnp.max(s, axis=0, keepdims=True))
        alphas.append(jnp.exp2(ms[h] - m_new))
        new_m.append(m_new)
        p_ref[slot, :rows, cols] = jnp.exp2(s - m_new).astype(p_ref.dtype)
    return new_m, jnp.concatenate(alphas, axis=1)


def _attend_tiles(carry, q_ref, tiles, s_ref, p_ref):
    ms, acc = carry
    q_all = q_ref[0].reshape(len(ms) * Q_BLOCK, HEAD_DIM)
    rows = [t[0].shape[0] for t in tiles]
    s_ref[0, :rows[0]] = _dot_nt(tiles[0][0], q_all)
    alpha_prev = None
    for i, (_, _, add, table) in enumerate(tiles):
        slot = i % 2
        if i + 1 < len(tiles):
            s_ref[1 - slot, :rows[i + 1]] = _dot_nt(tiles[i + 1][0], q_all)
        ms, alpha = _softmax_tile(slot, rows[i], ms, add, table, s_ref, p_ref)
        if i > 0:
            acc = acc * alpha_prev + _dot(tiles[i - 1][1], p_ref[1 - slot, :rows[i - 1]])
        alpha_prev = alpha
    last = len(tiles) - 1
    return ms, acc * alpha_prev + _dot(tiles[last][1], p_ref[last % 2, :rows[last]])


def _attend_init(heads, rows=V_ROWS):
    return ([jnp.full((1, Q_BLOCK), NEG, F32) for _ in range(heads)],
            jnp.zeros((rows, heads * Q_BLOCK), F32))


def _attend_finish(carry):
    _, acc = carry
    return acc[:HEAD_DIM] / acc[HEAD_DIM:HEAD_DIM + 1]


def _far_attend(qi, heads, q_ref, load_k, load_vt, load_add, last_start, last_add, last_table, s_ref, p_ref):
    T, C = Q_BLOCK, KEY_CHUNK
    pairs = (qi // 2 + 1) // 2
    q_all = q_ref[0].reshape(heads * T, HEAD_DIM)
    tile = lambda j: pl.multiple_of(jnp.where(j >= 2 * pairs, last_start, jnp.maximum(j, 0) * C), T)
    logits = lambda j: _dot_nt(load_k(tile(j)), q_all)
    s_ref[0] = logits(0)
    p_ref[1] = jnp.zeros(p_ref.shape[1:], p_ref.dtype)

    softmax = lambda slot, j, ms: _softmax_tile(slot, C, ms, load_add(tile(j)), None, s_ref, p_ref)

    def body(i, carry):
        ms, acc, alpha_prev = carry
        a = 2 * i
        s_ref[1] = logits(a + 1)
        ms, alpha_a = softmax(0, a, ms)
        acc = acc * alpha_prev + _dot(load_vt(tile(a - 1)), p_ref[1])
        s_ref[0] = logits(a + 2)
        ms, alpha_b = softmax(1, a + 1, ms)
        acc = acc * alpha_a + _dot(load_vt(tile(a)), p_ref[0])
        return ms, acc, alpha_b

    ms, acc = _attend_init(heads)
    ms, acc, alpha_prev = lax.fori_loop(0, pairs, body, (ms, acc, jnp.ones((1, heads * T), F32)))
    acc = acc * alpha_prev + _dot(load_vt(tile(2 * pairs - 1)), p_ref[1])
    ms, alpha_last = _softmax_tile(0, C, ms, last_add, last_table, s_ref, p_ref)
    return ms, acc * alpha_last + _dot(load_vt(last_start), p_ref[0])


def _nsa_kernel(q_ref, kc_ref, vcx_ref, ks_ref, vst_ref, kw_ref, vwt_ref, gate_ref, band_ref, tdiag_ref,
                o_ref, sc_ref, sel_ref, s_ref, p_ref):
    qi = pl.program_id(2)
    hg, T, C = NSA_GROUP, Q_BLOCK, KEY_CHUNK
    t0 = qi * T
    n_cp = kc_ref.shape[2]
    n_s = sel_ref.shape[1]
    row_c = lax.broadcasted_iota(I32, (C, T), 0)
    tpos_c = t0 + lax.broadcasted_iota(I32, (C, T), 1)
    diag_start = pl.multiple_of(jnp.maximum(t0 - T, 0), T)

    band0 = pl.multiple_of(jnp.maximum(t0 // CMP_STRIDE - 2 * SUBLANES, 0), SUBLANES)
    sc_ref[...] = jnp.where(lax.broadcasted_iota(I32, sc_ref.shape, 0) >= band0 + CMP_BAND, NEG, 0.0)
    sc_ref[pl.ds(band0, CMP_BAND), :] = band_ref[0]
    rows_c = min(C, n_cp)
    cmp_tiles = [(kc_ref[0, 0, lo:lo + rows_c, :], vcx_ref[0, 0, :, lo:lo + rows_c], None,
                  sc_ref[lo:lo + rows_c, :]) for lo in range(0, n_cp, rows_c)]
    m_c, acc_c = _attend_tiles(_attend_init(hg, vcx_ref.shape[2]), q_ref, cmp_tiles, s_ref, p_ref)
    seen = jnp.concatenate(m_c, axis=1) > 0.5 * NEG
    acc_c = acc_c * jnp.where(seen, 1.0 / acc_c[HEAD_DIM:HEAD_DIM + 1], 0.0)
    o_c = [acc_c[:HEAD_DIM, h * T:(h + 1) * T] for h in range(hg)]
    imp = acc_c[V_ROWS:, 0:T]
    for h in range(1, hg):
        imp = imp + acc_c[V_ROWS:, h * T:(h + 1) * T]

    blk = lax.broadcasted_iota(I32, (n_s, T), 0)
    blk_f = blk.astype(F32)
    cur = (t0 + lax.broadcasted_iota(I32, (n_s, T), 1)) // SLC_LEN
    forced = (blk == 0) | (blk == cur) | (blk == cur - 1)
    work = jnp.where(forced, -3e38, jnp.where(blk <= cur, imp, -1e6))

    def pick(_, carry):
        work, sel = carry
        mx = jnp.max(work, axis=0, keepdims=True)
        first = jnp.min(jnp.where(work == mx, blk_f, float(n_s)), axis=0, keepdims=True)
        hit = blk_f == first
        return jnp.where(hit, -3e38, work), jnp.where(hit, 1.0, sel)

    _, sel = lax.fori_loop(0, max(min(SLC_TOPN, n_s) - N_FORCED, 0), pick,
                           (work, jnp.where(forced, 1.0, 0.0)))
    sel_add = jnp.where(sel > 0.5, 0.0, NEG)
    sel_ref[0] = sel_add
    sel_ref[1] = jnp.where(blk < (t0 - T) // SLC_LEN, sel_add, NEG)

    def block_mask(start, which):
        first = start // SLC_LEN
        return jnp.concatenate([jnp.broadcast_to(sel_ref[which, pl.ds(first + i, 1), :], (SLC_LEN, T))
                                for i in range(C // SLC_LEN)], axis=0)

    load_ks = lambda start: ks_ref[0, 0, pl.ds(start, C), :]
    load_vst = lambda start: vst_ref[0, 0, :, pl.ds(start, C)]
    o_s = _attend_finish(_far_attend(qi, hg, q_ref, load_ks, load_vst, lambda start: block_mask(start, 1),
                                     diag_start, block_mask(diag_start, 0), tdiag_ref[0], s_ref, p_ref))

    def win_tile(lo, rows, last):
        start = pl.multiple_of(jnp.maximum(lo, 0), T)
        pos = row_c[:rows] + start
        rel = tpos_c[:rows] - pos
        add = jnp.where((pos < lo + rows) & (rel >= 0) & (rel < WINDOW), 0.0, NEG)
        return (kw_ref[0, 0, pl.ds(start, rows), :], vwt_ref[0, 0, :, pl.ds(start, rows)], add,
                tdiag_ref[0] if last else None)

    win_tiles = [win_tile(t0 - T - 2 * C, C, False), win_tile(t0 - T - C, C, False), win_tile(t0 - T, C, True)]
    o_w = _attend_finish(_attend_tiles(_attend_init(hg), q_ref, win_tiles, s_ref, p_ref))

    gates = gate_ref[0, 0]
    for h in range(hg):
        cols = slice(h * T, (h + 1) * T)
        o = (gates[3 * h:3 * h + 1] * o_c[h] + gates[3 * h + 1:3 * h + 2] * o_s[:, cols]
             + gates[3 * h + 2:3 * h + 3] * o_w[:, cols])
        o_ref[0, :, h * HEAD_DIM:(h + 1) * HEAD_DIM] = o.T.astype(o_ref.dtype)


def _nsa(q, kc, vcx, ks, vst, kw, vwt, gates_t, band, tdiag):
    b, _, s, _ = q.shape
    g, hg = NSA_KV_HEADS, NSA_GROUP
    nq = s // Q_BLOCK
    n_cp = kc.shape[2]
    assert n_cp % min(KEY_CHUNK, n_cp) == 0
    rows = lambda r: pl.BlockSpec((1, 1, r, HEAD_DIM), lambda bi, gi, qi: (bi, gi, 0, 0))
    cols = lambda r, c: pl.BlockSpec((1, 1, r, c), lambda bi, gi, qi: (bi, gi, 0, 0))
    return pl.pallas_call(
        _nsa_kernel,
        grid=(b, g, nq),
        in_specs=[
            pl.BlockSpec((1, hg, Q_BLOCK, HEAD_DIM), lambda bi, gi, qi: (bi, gi, qi, 0)),
            rows(n_cp), cols(vcx.shape[2], n_cp), rows(s), cols(V_ROWS, s), rows(s), cols(V_ROWS, s),
            pl.BlockSpec((1, 1, hg * 3, Q_BLOCK), lambda bi, gi, qi: (bi, gi, 0, qi)),
            pl.BlockSpec((1, CMP_BAND, hg * Q_BLOCK), lambda bi, gi, qi: (jnp.minimum(qi, 2), 0, gi)),
            pl.BlockSpec((1, KEY_CHUNK, hg * Q_BLOCK), lambda bi, gi, qi: (jnp.minimum(qi, 1), 0, gi)),
        ],
        out_specs=pl.BlockSpec((1, Q_BLOCK, hg * HEAD_DIM), lambda bi, gi, qi: (bi, qi, gi)),
        out_shape=jax.ShapeDtypeStruct((b, s, NSA_HEADS * HEAD_DIM), BF16),
        scratch_shapes=[pltpu.VMEM((n_cp, hg * Q_BLOCK), F32), pltpu.VMEM((2, s // SLC_LEN, Q_BLOCK), F32),
                        pltpu.VMEM((2, KEY_CHUNK, hg * Q_BLOCK), F32), pltpu.VMEM((2, KEY_CHUNK, hg * Q_BLOCK), BF16)],
        compiler_params=_params("parallel", "parallel", "arbitrary"),
        name="nsa",
    )(q, kc, vcx, ks, vst, kw, vwt, gates_t, band, tdiag)


DIGIT_BITS = 14
DIGIT_BIAS = 0x80
DIGIT_DTYPE = jnp.bfloat16


def _pack_patterns(p):
    half = p.shape[0] // 2
    return (p[:half] << 16) | p[half:]


def _sortable_key(x):
    b = lax.bitcast_convert_type(jnp.where(x == 0.0, 0.0, x), I32)
    return b ^ ((b >> 31) & 0x7FFFFFFF)


def _dsa_kernel(q_ref, iq_ref, iw_ref, ik_ref, k_ref, vt_ref, tdiag_ref, o_ref, key_ref, dig_ref, thr_ref, s_ref, p_ref,
                *, n_keep):
    qi = pl.program_id(1)
    H, T, C = DSA_HEADS, Q_BLOCK, KEY_CHUNK
    t0 = qi * T
    n_chunks = qi // 2 + 1
    iw = iw_ref[0]
    row = lax.broadcasted_iota(I32, (C, T), 0)
    tpos = t0 + lax.broadcasted_iota(I32, (C, T), 1)
    int_min = jnp.iinfo(jnp.int32).min

    def score_step(j, _):
        for u in range(2):
            start = pl.multiple_of((2 * j + u) * C, C)
            ikc = ik_ref[0, pl.ds(start, C), :]
            score = jnp.maximum(_dot_nt(ikc, iq_ref[0, 0]), 0.0) * iw[0:1]
            for h in range(1, IDX_HEADS):
                score = score + jnp.maximum(_dot_nt(ikc, iq_ref[0, h]), 0.0) * iw[h:h + 1]
            score = jnp.where(row + start <= tpos, score, NEG)
            key = _sortable_key(score)
            key_ref[pl.ds(start, C), :] = key
            digit = lax.shift_right_logical(key ^ int_min, jnp.int32(32 - DIGIT_BITS))
            dig_ref[pl.ds(pl.multiple_of(start // 2, C // 2), C // 2), :] = _pack_patterns(digit + DIGIT_BIAS)
        return 0

    lax.fori_loop(0, (n_chunks + 1) // 2, score_step, 0)

    def count_digits(cand_digit):
        cand = cand_digit + DIGIT_BIAS
        cand = pltpu.bitcast(jnp.broadcast_to((cand << 16) | cand, (SUBLANES, T)), DIGIT_DTYPE)

        def tile_hits(j):
            words = dig_ref[pl.ds(pl.multiple_of(j * (C // 2), C // 2), C // 2), :]
            vals = pltpu.bitcast(words, DIGIT_DTYPE).reshape(C // (2 * SUBLANES), 2 * SUBLANES, T)
            hits = jnp.where(vals >= cand[None], jnp.ones((), DIGIT_DTYPE), jnp.zeros((), DIGIT_DTYPE))
            parts = [hits[i] for i in range(hits.shape[0])]
            while len(parts) > 1:
                parts = [parts[i] + parts[i + 1] for i in range(0, len(parts), 2)]
            return parts[0].astype(F32)

        def body(i, acc):
            second = 2 * i + 1
            keep = jnp.where(second < n_chunks, 1.0, 0.0)
            return acc + tile_hits(2 * i) + tile_hits(jnp.minimum(second, n_chunks - 1)) * keep

        acc = lax.fori_loop(0, (n_chunks + 1) // 2, body, jnp.zeros((2 * SUBLANES, T), F32))
        return jnp.sum(acc, axis=0, keepdims=True).astype(I32)

    def digit_search(need, n_all):
        def bit(i, carry):
            digit, n_at = carry
            cand = digit | lax.shift_left(jnp.int32(1), DIGIT_BITS - 1 - i)
            n_ge = count_digits(cand)
            ok = n_ge >= need
            return jnp.where(ok, cand, digit), jnp.where(ok, n_ge, n_at)

        return lax.fori_loop(0, DIGIT_BITS, bit, (jnp.zeros((1, T), I32), n_all))

    def count(pred_fn):
        def body(j, acc):
            start = pl.multiple_of(j * C, C)
            hits = jnp.where(pred_fn(key_ref[pl.ds(start, C), :], start), 1.0, 0.0)
            return acc + jnp.sum(hits.reshape(C // COUNT_ROWS, COUNT_ROWS, T), axis=0)
        acc = lax.fori_loop(0, n_chunks, body, jnp.zeros((COUNT_ROWS, T), F32))
        return jnp.sum(acc, axis=0, keepdims=True).astype(I32)

    d1, n_at1 = digit_search(n_keep, jnp.full((1, T), 1, I32) * (n_chunks * C))
    n_gt1 = count_digits(d1 + 1)

    def second_digits(j, _):
        start = pl.multiple_of(j * C, C)
        u = key_ref[pl.ds(start, C), :] ^ int_min
        digit = lax.shift_right_logical(u, jnp.int32(32 - 2 * DIGIT_BITS)) & (2 ** DIGIT_BITS - 1)
        first = lax.shift_right_logical(u, jnp.int32(32 - DIGIT_BITS))
        pattern = jnp.where(first == d1, digit + DIGIT_BIAS, 0)
        dig_ref[pl.ds(pl.multiple_of(start // 2, C // 2), C // 2), :] = _pack_patterns(pattern)
        return 0

    lax.fori_loop(0, n_chunks, second_digits, 0)
    d2, n_at2 = digit_search(n_keep - n_gt1, n_at1 - n_gt1)

    def thr_bit(i, carry):
        thr_u, n_at = carry
        cand_u = thr_u | lax.shift_left(jnp.int32(1), 31 - i)
        cand = cand_u ^ int_min
        n_ge = count(lambda keys, _: keys >= cand)
        ok = n_ge >= n_keep
        return jnp.where(ok, cand_u, thr_u), jnp.where(ok, n_ge, n_at)

    prefix = lax.shift_left(d1, jnp.int32(32 - DIGIT_BITS)) | lax.shift_left(d2, jnp.int32(32 - 2 * DIGIT_BITS))
    n_prefix = n_gt1 + n_at2
    thr_ref[...] = prefix

    @pl.when(jnp.max((n_prefix - n_keep).astype(F32)) > 0.0)
    def _():
        thr_u, n_at = lax.fori_loop(2 * DIGIT_BITS, 32, thr_bit, (prefix, n_prefix))
        thr_ref[...] = thr_u
        thr = thr_u ^ int_min

        @pl.when(jnp.max((n_at - n_keep).astype(F32)) > 0.0)
        def _():
            need = n_keep - count(lambda keys, _: keys > thr)
            n_pos_bits = int(math.ceil(math.log2(key_ref.shape[0] + 1)))

            def pos_bit(i, lim):
                cand = lim + lax.shift_left(jnp.int32(1), n_pos_bits - 1 - i)
                n_tied = count(lambda keys, start: (keys == thr) & (row + start < cand))
                return jnp.where(n_tied <= need, cand, lim)

            lim = lax.fori_loop(0, n_pos_bits, pos_bit, jnp.zeros((1, T), I32))

            def drop(j, _):
                start = pl.multiple_of(j * C, C)
                keys = key_ref[pl.ds(start, C), :]
                key_ref[pl.ds(start, C), :] = jnp.where((keys == thr) & (row + start >= lim), int_min, keys)
                return 0

            lax.fori_loop(0, n_chunks, drop, 0)

    thr = thr_ref[...] ^ int_min

    def key_mask(start, far):
        keep = key_ref[pl.ds(start, C), :] >= thr
        if far:
            keep = keep & (row + start < t0 - T)
        return jnp.where(keep, 0.0, NEG)

    load_k = lambda start: k_ref[0, pl.ds(start, C), :]
    load_vt = lambda start: vt_ref[0, :, pl.ds(start, C)]
    diag_start = pl.multiple_of(jnp.maximum(t0 - T, 0), T)
    o = _attend_finish(_far_attend(qi, H, q_ref, load_k, load_vt, lambda start: key_mask(start, True),
                                   diag_start, key_mask(diag_start, False), tdiag_ref[0], s_ref, p_ref))
    for h in range(H):
        o_ref[0, :, h * HEAD_DIM:(h + 1) * HEAD_DIM] = o[:, h * T:(h + 1) * T].T.astype(o_ref.dtype)


def _dsa(q, iq, iw_t, ik, k, vt, tdiag):
    b, _, s, _ = q.shape
    assert s % (2 * KEY_CHUNK) == 0
    nq = s // Q_BLOCK
    n_keep = min(DSA_TOPK_MAX, s // 4)
    heads = lambda: pl.BlockSpec((1, DSA_HEADS, Q_BLOCK, HEAD_DIM), lambda bi, qi: (bi, 0, qi, 0))
    rows = lambda: pl.BlockSpec((1, s, HEAD_DIM), lambda bi, qi: (bi, 0, 0))
    return pl.pallas_call(
        functools.partial(_dsa_kernel, n_keep=n_keep),
        grid=(b, nq),
        in_specs=[heads(), heads(),
                  pl.BlockSpec((1, IDX_HEADS, Q_BLOCK), lambda bi, qi: (bi, 0, qi)),
                  rows(), rows(),
                  pl.BlockSpec((1, V_ROWS, s), lambda bi, qi: (bi, 0, 0)),
                  pl.BlockSpec((1, KEY_CHUNK, DSA_HEADS * Q_BLOCK), lambda bi, qi: (jnp.minimum(qi, 1), 0, 1))],
        out_specs=pl.BlockSpec((1, Q_BLOCK, DSA_HEADS * HEAD_DIM), lambda bi, qi: (bi, qi, 0)),
        out_shape=jax.ShapeDtypeStruct((b, s, DSA_HEADS * HEAD_DIM), BF16),
        scratch_shapes=[pltpu.VMEM((s, Q_BLOCK), I32), pltpu.VMEM((s // 2, Q_BLOCK), I32),
                        pltpu.VMEM((1, Q_BLOCK), I32),
                        pltpu.VMEM((2, KEY_CHUNK, DSA_HEADS * Q_BLOCK), F32),
                        pltpu.VMEM((2, KEY_CHUNK, DSA_HEADS * Q_BLOCK), BF16)],
        compiler_params=_params("parallel", "arbitrary"),
        name="dsa",
    )(q, iq, iw_t, ik, k, vt, tdiag)


def _merge_kernel(x_ref, oa_ref, ob_ref, ga_ref, gb_ref, wa_ref, wb_ref, wo_ref, gn_ref, wr_ref, br_ref,
                  x1_ref, xn_ref, gate_ref, gate_t_ref):
    merged = ga_ref[...] * _dot(oa_ref[0], wa_ref[...]) + gb_ref[...] * _dot(ob_ref[0], wb_ref[...])
    x1 = x_ref[...] + _dot(merged.astype(BF16), wo_ref[...])
    x1_ref[...] = x1
    xn = _row_rms(x1, gn_ref[...]).astype(BF16)
    xn_ref[...] = xn

    logits = _dot(xn, wr_ref[...]) + br_ref[...]
    lane = lax.broadcasted_iota(I32, logits.shape, 1)
    work = jnp.where(lane < N_EXPERTS, logits, -3e38)
    gates = jnp.zeros_like(logits)
    denom = jnp.zeros((logits.shape[0], 1), F32)
    top = None
    for _ in range(MOE_TOPK):
        mx = jnp.max(work, axis=1, keepdims=True)
        first = jnp.min(jnp.where(work == mx, lane, LANES), axis=1, keepdims=True)
        hit = lane == first
        top = mx if top is None else top
        e = jnp.exp(mx - top)
        gates = jnp.where(hit, e, gates)
        denom = denom + e
        work = jnp.where(hit, -3e38, work)
    gates = gates / denom
    gate_ref[...] = gates
    gate_t_ref[...] = gates.T


def _merge(x, oa, ob, ga, gb, w_branch_a, w_branch_b, w_out, ffn_norm, w_router, b_router, tm):
    b, s, d = x.shape
    n = b * s
    xf = x.reshape(n, d)
    nt = s // tm
    wa = w_branch_a.astype(BF16)
    wb = w_branch_b.astype(BF16)
    wr = jnp.concatenate([w_router, jnp.zeros((d, LANES - N_EXPERTS), F32)], axis=1).astype(BF16)
    br = jnp.concatenate([b_router, jnp.zeros((LANES - N_EXPERTS,), F32)]).reshape(1, LANES)
    consts = [wa, wb, w_out.astype(BF16), ffn_norm.reshape(1, d), wr, br]
    row = lambda w: pl.BlockSpec((tm, w), lambda bi, ti: (bi * nt + ti, 0))
    heads = lambda: pl.BlockSpec((1, tm, NSA_HEADS * HEAD_DIM), lambda bi, ti: (bi, ti, 0))
    return pl.pallas_call(
        _merge_kernel,
        grid=(b, nt),
        in_specs=[row(d), heads(), heads(), row(d), row(d)] + [_const_spec(c.shape) for c in consts],
        out_specs=[row(d), row(d), row(LANES), pl.BlockSpec((LANES, tm), lambda bi, ti: (0, bi * nt + ti))],
        out_shape=[jax.ShapeDtypeStruct((n, d), F32), jax.ShapeDtypeStruct((n, d), BF16),
                   jax.ShapeDtypeStruct((n, LANES), F32), jax.ShapeDtypeStruct((LANES, n), F32)],
        compiler_params=_params("parallel", "parallel"),
        name="merge",
    )(xf, oa, ob, ga, gb, *consts)


def _deinterleave_index(width):
    half = MXU_DIM // 2
    c = np.arange(width)
    r = c % MXU_DIM
    return (c - r) + np.where(r < half, 2 * r, 2 * (r - half) + 1)


def _moe_prep_kernel(w_ref, perm_ref, o_ref):
    perm = perm_ref[...]
    for g in range(w_ref.shape[2] // MXU_DIM):
        cols = slice(g * MXU_DIM, (g + 1) * MXU_DIM)
        o_ref[0, :, cols] = _dot(w_ref[0, :, cols].astype(BF16), perm).astype(BF16)


def _moe_prep(w_gate_up, tr):
    e, d, f2 = w_gate_up.shape
    src = _deinterleave_index(MXU_DIM)
    perm = jnp.asarray((np.arange(MXU_DIM)[:, None] == src[None, :]).astype(np.float32), BF16)
    return pl.pallas_call(
        _moe_prep_kernel,
        grid=(e, d // tr),
        in_specs=[pl.BlockSpec((1, tr, f2), lambda i, j: (i, j, 0)), _const_spec(perm.shape)],
        out_specs=pl.BlockSpec((1, tr, f2), lambda i, j: (i, j, 0)),
        out_shape=jax.ShapeDtypeStruct((e, d, f2), BF16),
        compiler_params=_params("parallel", "parallel"),
        name="moe_prep",
    )(w_gate_up, perm)


def _moe_kernel(x1_ref, xn_ref, gate_ref, gate_t_ref, lower_ref, upper_ref, wgu_ref, bgu_ref, wd_ref, bd_ref,
                o_ref, rank_hi_ref, rank_lo_ref, rank_t_ref):
    e = pl.program_id(1)
    tm, R = xn_ref.shape[0], MOE_ROWS

    @pl.when(e == 0)
    def _():
        o_ref[...] = x1_ref[...]
        routed = jnp.where(gate_ref[...] > 0.0, 1.0, 0.0).astype(BF16)
        routed_t = jnp.where(gate_t_ref[...] > 0.0, 1.0, 0.0).astype(BF16)
        rank = jnp.where(gate_ref[...] > 0.0, _dot(lower_ref[...], routed), -1.0)
        hi = rank.astype(BF16)
        rank_hi_ref[...] = hi
        rank_lo_ref[...] = (rank - hi.astype(F32)).astype(BF16)
        rank_t_ref[...] = _dot(routed_t, upper_ref[...])

    g_row = gate_t_ref[pl.ds(e, 1), :]
    rank_row = jnp.where(g_row > 0.0, rank_t_ref[pl.ds(e, 1), :], -1.0)
    take_e = jnp.where(lax.broadcasted_iota(I32, (LANES, LANES), 0) == e, 1.0, 0.0).astype(BF16)
    rank_rep = _dot(rank_hi_ref[...], take_e) + _dot(rank_lo_ref[...], take_e)
    rank_cols = jnp.concatenate([rank_rep] * (R // LANES) + [rank_rep[:, :R % LANES]], axis=1)
    n_routed = jnp.sum(jnp.where(g_row > 0.0, 1.0, 0.0)).astype(I32)
    slot_r = lax.broadcasted_iota(I32, (R, tm), 0).astype(F32)
    slot_c = lax.broadcasted_iota(I32, (tm, R), 1).astype(F32)
    half = MXU_DIM // 2

    def block(j, _):
        base = (j * R).astype(F32)
        pick = jnp.where(rank_row - base == slot_r, 1.0, 0.0)
        xg = _dot(pick.astype(BF16), xn_ref[...]).astype(BF16)
        hu = _dot(xg, wgu_ref[0]) + bgu_ref[0]
        acts = []
        for g in range(hu.shape[1] // MXU_DIM):
            glu = jnp.minimum(hu[:, g * MXU_DIM:g * MXU_DIM + half], SWIGLU_LIMIT)
            lin = jnp.clip(hu[:, g * MXU_DIM + half:(g + 1) * MXU_DIM], -SWIGLU_LIMIT, SWIGLU_LIMIT)
            acts.append((glu * jax.nn.sigmoid(SWIGLU_ALPHA * glu) * (lin + 1.0)).astype(BF16))
        y = _dot(jnp.concatenate(acts, axis=1), wd_ref[0]) + bd_ref[0]
        g_r = jnp.sum(pick * g_row, axis=1, keepdims=True)
        put = jnp.where(rank_cols - base == slot_c, 1.0, 0.0).astype(BF16)
        o_ref[...] += _dot(put, (y * g_r).astype(BF16))
        return 0

    lax.fori_loop(0, (n_routed + R - 1) // R, block, 0)


def _moe(x1, xn, gates, gates_t, w_gate_up, b_gate_up, w_down, b_down, tm):
    n, d = x1.shape
    f = D_EXPERT
    wgu = _moe_prep(w_gate_up, tr=256)
    bgu = b_gate_up[:, _deinterleave_index(2 * f)].reshape(N_EXPERTS, 1, 2 * f)
    wd = w_down.astype(BF16)
    bd = b_down.reshape(N_EXPERTS, 1, d)
    earlier = np.arange(tm)[None, :] < np.arange(tm)[:, None]
    lower = jnp.asarray(earlier.astype(np.float32), BF16)
    upper = jnp.asarray(earlier.T.astype(np.float32), BF16)
    row = lambda w: pl.BlockSpec((tm, w), lambda i, e: (i, 0))
    per_e = lambda a, b_: pl.BlockSpec((1, a, b_), lambda i, e: (e, 0, 0))
    return pl.pallas_call(
        _moe_kernel,
        grid=(n // tm, N_EXPERTS),
        in_specs=[row(d), row(d), row(LANES), pl.BlockSpec((LANES, tm), lambda i, e: (0, i)),
                  _const_spec(lower.shape), _const_spec(upper.shape),
                  per_e(d, 2 * f), per_e(1, 2 * f), per_e(f, d), per_e(1, d)],
        out_specs=row(d),
        out_shape=jax.ShapeDtypeStruct((n, d), F32),
        scratch_shapes=[pltpu.VMEM((tm, LANES), BF16), pltpu.VMEM((tm, LANES), BF16), pltpu.VMEM((LANES, tm), F32)],
        compiler_params=_params("parallel", "arbitrary"),
        name="moe",
    )(x1, xn, gates, gates_t, lower, upper, wgu, bgu, wd, bd)


def _rel_bucket(rel):
    n = jnp.maximum(rel, 0)
    exact = REL_BUCKETS // 2
    nf = jnp.maximum(n, 1).astype(F32)
    large = exact + (jnp.log(nf / exact) / math.log(REL_MAX_DIST / exact) * (REL_BUCKETS - exact)).astype(I32)
    large = jnp.minimum(large, REL_BUCKETS - 1)
    return jnp.where(n < exact, n, large)


def _rel_table(rel_bias, rel):
    far = rel_bias[REL_BUCKETS - 1]
    tab = jnp.where(rel[..., None] >= 0, (rel_bias[_rel_bucket(rel)] - far) * LOG2E, NEG)
    tab = jnp.moveaxis(tab, -1, -2)
    return tab.reshape(tab.shape[:-2] + (-1,)).astype(F32)


def _bias_tables(rel_bias):
    i = np.arange(Q_BLOCK)[None, :]
    j = np.arange(KEY_CHUNK)[:, None]
    tdiag = _rel_table(rel_bias, jnp.asarray(np.stack([i - j, i - j + Q_BLOCK])))
    r = np.arange(CMP_BAND)[:, None]
    rel_band = np.stack([i - (CMP_LEN - 1) - CMP_STRIDE * (c0 + r) for c0 in (0, -SUBLANES, -2 * SUBLANES)])
    band = _rel_table(rel_bias[:, :NSA_HEADS], jnp.asarray(rel_band))
    return tdiag, band


def _overlap_matrix_t(s):
    n_cp = s // CMP_STRIDE
    n_s = s // SLC_LEN
    c_start = np.arange(n_cp) * CMP_STRIDE
    s_start = np.arange(n_s) * SLC_LEN
    ovl = (c_start[None, :] < s_start[:, None] + SLC_LEN) & (c_start[None, :] + CMP_LEN > s_start[:, None])
    ovl[:, n_cp - 1] = False
    return jnp.asarray(ovl.astype(np.float32), BF16)


def _layer(x, attn_norm, w_in, nsa_q_norm, nsa_k_norm, cmp_k_pe, cmp_k_w1, cmp_k_b1, cmp_k_w2,
           cmp_v_pe, cmp_v_w1, cmp_v_b1, cmp_v_w2, dsa_kv_norm, dsa_w_uk, dsa_w_uv, dsa_q_norm,
           dsa_k_norm, w_branch_a, w_branch_b, w_out, rel_bias, ffn_norm, w_router, b_router,
           w_gate_up, b_gate_up, w_down, b_down):
    b, s, d = x.shape
    n = b * s
    g = NSA_KV_HEADS
    (qa, ck, cv, ks, sv_t, kw, wv_t, misc_t, qb, kb, vb_t, iq, ik, ga, gb) = _in_proj(
        x, attn_norm, w_in, nsa_q_norm, nsa_k_norm, dsa_q_norm, dsa_kv_norm, dsa_w_uk, dsa_w_uv, dsa_k_norm,
        tm=min(256, s))

    chunks = lambda a: a.reshape(b, s // CMP_STRIDE, CMP_STRIDE, g, HEAD_DIM).transpose(0, 3, 1, 2, 4).reshape(
        b, g, s // CMP_STRIDE, CMP_STRIDE * HEAD_DIM)
    kc = _compress(chunks(ck), cmp_k_pe, cmp_k_w1, cmp_k_b1, cmp_k_w2, nsa_k_norm, True)
    vc = _compress(chunks(cv), cmp_v_pe, cmp_v_w1, cmp_v_b1, cmp_v_w2, nsa_k_norm, False)

    tdiag, band = _bias_tables(rel_bias)
    gates_t = misc_t[:, :NSA_HEADS * 3].reshape(b, g, NSA_GROUP * 3, s)
    overlap_t = _overlap_matrix_t(s)
    vc_t = vc.transpose(0, 1, 3, 2)
    vcx = jnp.concatenate([vc_t, jnp.ones((b, g, V_ROWS - HEAD_DIM, vc_t.shape[3]), BF16),
                           jnp.broadcast_to(overlap_t, (b, g) + overlap_t.shape)], axis=2)
    oa = _nsa(qa, kc, vcx, ks, sv_t, kw, wv_t, gates_t, band, tdiag)

    iw_t = misc_t[:, NSA_HEADS * 3:NSA_HEADS * 3 + IDX_HEADS]
    ob = _dsa(qb, iq, iw_t, ik.reshape(b, s, IDX_DIM), kb.reshape(b, s, HEAD_DIM), vb_t, tdiag)

    x1, xn, moe_gates, moe_gates_t = _merge(x, oa, ob, ga, gb, w_branch_a, w_branch_b, w_out, ffn_norm, w_router,
                               b_router, tm=min(512, s))
    out = _moe(x1, xn, moe_gates, moe_gates_t, w_gate_up, b_gate_up, w_down, b_down, tm=min(MOE_TOKENS, n))
    return out.reshape(b, s, d)


def kernel(x, attn_norm, w_in, nsa_q_norm, nsa_k_norm, cmp_k_pe, cmp_k_w1, cmp_k_b1, cmp_k_w2, cmp_v_pe, cmp_v_w1, cmp_v_b1, cmp_v_w2, dsa_kv_norm, dsa_w_uk, dsa_w_uv, dsa_q_norm, dsa_k_norm, w_branch_a, w_branch_b, w_out, rel_bias, ffn_norm, w_router, b_router, w_gate_up, b_gate_up, w_down, b_down):
    for l in range(attn_norm.shape[0]):
        x = _layer(x, attn_norm[l], w_in[l], nsa_q_norm[l], nsa_k_norm[l], cmp_k_pe[l], cmp_k_w1[l],
                   cmp_k_b1[l], cmp_k_w2[l], cmp_v_pe[l], cmp_v_w1[l], cmp_v_b1[l], cmp_v_w2[l],
                   dsa_kv_norm[l], dsa_w_uk[l], dsa_w_uv[l], dsa_q_norm[l], dsa_k_norm[l],
                   w_branch_a[l], w_branch_b[l], w_out[l], rel_bias, ffn_norm[l], w_router[l],
                   b_router[l], w_gate_up[l], b_gate_up[l], w_down[l], b_down[l])
    return x
```

```python
import functools
import math

import numpy as np
import jax
import jax.numpy as jnp
from jax import lax
from jax.experimental import pallas as pl
from jax.experimental.pallas import tpu as pltpu

D_MODEL = 1024
HEAD_DIM = 64
NSA_HEADS = 8
NSA_KV_HEADS = 2
NSA_GROUP = NSA_HEADS // NSA_KV_HEADS
CMP_LEN = 32
CMP_STRIDE = 16
CMP_HID = 128
SLC_LEN = 64
SLC_TOPN = 16
WINDOW = 512
DSA_HEADS = 8
DSA_KV_RANK = 128
IDX_HEADS = 8
IDX_DIM = 64
DSA_TOPK_MAX = 256
N_EXPERTS = 32
MOE_TOPK = 4
D_EXPERT = 1024
SWIGLU_LIMIT = 7.0
SWIGLU_ALPHA = 1.702
REL_BUCKETS = 32
REL_MAX_DIST = 128
NORM_EPS = 1e-6
NEG = -1e30
LOG2E = math.log2(math.e)

Q_BLOCK = 128
KEY_CHUNK = 256
N_FORCED = 3
CMP_BAND = 24
COUNT_ROWS = 32
LANES = 128
SUBLANES = 8
MXU_DIM = 256
V_ROWS = HEAD_DIM + SUBLANES
MOE_TOKENS = 1024
MOE_ROWS = 160
VMEM_LIMIT = 56 * 1024 * 1024

F32 = jnp.float32
BF16 = jnp.bfloat16
I32 = jnp.int32


def _params(*sem):
    return pltpu.CompilerParams(dimension_semantics=sem, vmem_limit_bytes=VMEM_LIMIT)


def _dot(a, b):
    return jnp.dot(a, b, preferred_element_type=F32)


def _dot_nt(a, b):
    return lax.dot_general(a, b, (((1,), (1,)), ((), ())), preferred_element_type=F32)


def _const_spec(shape):
    nd = len(shape)
    return pl.BlockSpec(shape, lambda *_: (0,) * nd)


def _segment_sumsq(y, seg_ones):
    y2 = y * y
    hi = y2.astype(BF16)
    lo = (y2 - hi.astype(F32)).astype(BF16)
    return _dot(hi, seg_ones) + _dot(lo, seg_ones)


def _head_rms(y, seg_ones, gain):
    ms = _segment_sumsq(y, seg_ones) * (1.0 / HEAD_DIM)
    return y * lax.rsqrt(ms + NORM_EPS) * gain


def _row_rms(y, gain):
    ms = jnp.mean(y * y, axis=-1, keepdims=True)
    return y * lax.rsqrt(ms + NORM_EPS) * gain


def _in_proj_kernel(x_ref, g_ref, wqa_ref, wkv_ref, wmisc_ref, wqb_ref, wckv_ref, wiq_ref, wik_ref,
                    wga_ref, wgb_ref, seg512_ref, seg128_ref, gqa_ref, gka_ref, gqb_ref, gckv_ref,
                    wuk_ref, wuv_ref, gkb_ref,
                    qa_ref, ck_ref, cv_ref, ks_ref, sv_ref, kw_ref, wv_ref, misc_ref, qb_ref,
                    kb_ref, vb_ref, iq_ref, ik_ref, ga_ref, gb_ref):
    h = _row_rms(x_ref[...], g_ref[...]).astype(BF16)
    scale = HEAD_DIM ** -0.5 * LOG2E
    seg512 = seg512_ref[...]
    seg128 = seg128_ref[...]

    def put_heads(o_ref, y):
        for i in range(o_ref.shape[1]):
            o_ref[0, i] = y[:, i * HEAD_DIM:(i + 1) * HEAD_DIM].astype(o_ref.dtype)

    def put_values_t(o_ref, y):
        y_t = y.T
        ones = jnp.ones((V_ROWS - HEAD_DIM, y.shape[0]), o_ref.dtype)
        for i in range(o_ref.shape[1]):
            o_ref[0, i, :HEAD_DIM, :] = y_t[i * HEAD_DIM:(i + 1) * HEAD_DIM].astype(o_ref.dtype)
            o_ref[0, i, HEAD_DIM:, :] = ones

    qa = _head_rms(_dot(h, wqa_ref[...]), seg512, gqa_ref[...])
    put_heads(qa_ref, qa * scale)

    kv = _dot(h, wkv_ref[...])
    put_heads(ck_ref, kv[:, 0:128])
    put_heads(cv_ref, kv[:, 128:256])
    put_heads(ks_ref, _head_rms(kv[:, 256:384], seg128, gka_ref[...]))
    put_values_t(sv_ref, kv[:, 384:512])
    put_heads(kw_ref, _head_rms(kv[:, 512:640], seg128, gka_ref[...]))
    put_values_t(wv_ref, kv[:, 640:768])

    misc = _dot(h, wmisc_ref[...])
    lane = lax.broadcasted_iota(I32, misc.shape, 1)
    iw_scale = (IDX_HEADS ** -0.5)
    idx_scale = IDX_DIM ** -0.5
    misc = jnp.where(lane < NSA_HEADS * 3, jax.nn.sigmoid(misc), (misc * iw_scale) * idx_scale)
    misc_ref[0] = misc.T[:misc_ref.shape[1]]

    qb = _head_rms(_dot(h, wqb_ref[...]), seg512, gqb_ref[...])
    put_heads(qb_ref, qb * scale)

    ckv = _row_rms(_dot(h, wckv_ref[...]), gckv_ref[...]).astype(BF16)
    kb_ref[...] = _row_rms(_dot(ckv, wuk_ref[...]), gkb_ref[...]).astype(BF16)
    vb = _dot(ckv, wuv_ref[...])
    vb_ref[0, :HEAD_DIM, :] = vb.T.astype(BF16)
    vb_ref[0, HEAD_DIM:, :] = jnp.ones((V_ROWS - HEAD_DIM, vb.shape[0]), BF16)

    put_heads(iq_ref, _dot(h, wiq_ref[...]))
    ik_ref[...] = _dot(h, wik_ref[...]).astype(BF16)
    ga_ref[...] = jax.nn.sigmoid(_dot(h, wga_ref[...]))
    gb_ref[...] = jax.nn.sigmoid(_dot(h, wgb_ref[...]))


def _in_proj(x, attn_norm, w_in, nsa_q_norm, nsa_k_norm, dsa_q_norm, dsa_kv_norm, dsa_w_uk, dsa_w_uv,
             dsa_k_norm, tm):
    b, s, _ = x.shape
    n = b * s
    xf = x.reshape(n, -1)
    kvw = NSA_KV_HEADS * HEAD_DIM
    widths = (NSA_HEADS * HEAD_DIM, kvw, kvw, kvw, kvw, kvw, kvw, NSA_HEADS * 3,
              DSA_HEADS * HEAD_DIM, DSA_KV_RANK, IDX_HEADS * IDX_DIM, IDX_DIM, IDX_HEADS, D_MODEL, D_MODEL)
    offs = np.concatenate([[0], np.cumsum(widths)])
    sec = [w_in[:, offs[i]:offs[i + 1]] for i in range(len(widths))]
    wb = lambda w: w.astype(BF16)
    wqa = wb(sec[0])
    wkv = wb(jnp.concatenate(sec[1:7], axis=1))
    wmisc = wb(jnp.concatenate([sec[7], sec[12], jnp.zeros((D_MODEL, LANES - 32), F32)], axis=1))
    wqb, wckv, wiq, wik, wga, wgb = wb(sec[8]), wb(sec[9]), wb(sec[10]), wb(sec[11]), wb(sec[13]), wb(sec[14])

    def seg_ones(w):
        i = np.arange(w) // HEAD_DIM
        return jnp.asarray((i[:, None] == i[None, :]).astype(np.float32), BF16)

    consts = [attn_norm.reshape(1, -1), wqa, wkv, wmisc, wqb, wckv, wiq, wik, wga, wgb,
              seg_ones(512), seg_ones(128),
              jnp.tile(nsa_q_norm, NSA_HEADS).reshape(1, -1), jnp.tile(nsa_k_norm, NSA_KV_HEADS).reshape(1, -1),
              jnp.tile(dsa_q_norm, DSA_HEADS).reshape(1, -1), dsa_kv_norm.reshape(1, -1),
              wb(dsa_w_uk), wb(dsa_w_uv), dsa_k_norm.reshape(1, -1)]
    nt = s // tm
    g = NSA_KV_HEADS
    rows = lambda w, dt: (jax.ShapeDtypeStruct((n, w), dt), pl.BlockSpec((tm, w), lambda bi, ti: (bi * nt + ti, 0)))
    heads = lambda hh, dt=BF16: (jax.ShapeDtypeStruct((b, hh, s, HEAD_DIM), dt),
                                 pl.BlockSpec((1, hh, tm, HEAD_DIM), lambda bi, ti: (bi, 0, ti, 0)))
    values_t = (jax.ShapeDtypeStruct((b, g, V_ROWS, s), BF16),
                pl.BlockSpec((1, g, V_ROWS, tm), lambda bi, ti: (bi, 0, 0, ti)))
    cols = lambda r, dt: (jax.ShapeDtypeStruct((b, r, s), dt), pl.BlockSpec((1, r, tm), lambda bi, ti: (bi, 0, ti)))
    outs = [heads(NSA_HEADS), heads(g, F32), heads(g, F32), heads(g), values_t, heads(g), values_t,
            cols(32, F32), heads(DSA_HEADS), rows(HEAD_DIM, BF16), cols(V_ROWS, BF16), heads(IDX_HEADS),
            rows(IDX_DIM, BF16), rows(D_MODEL, F32), rows(D_MODEL, F32)]
    return pl.pallas_call(
        _in_proj_kernel,
        grid=(b, nt),
        in_specs=[rows(D_MODEL, F32)[1]] + [_const_spec(c.shape) for c in consts],
        out_specs=[spec for _, spec in outs],
        out_shape=[shape for shape, _ in outs],
        compiler_params=_params("parallel", "parallel"),
        name="in_proj",
    )(xf, *consts)


def _compress_kernel(c_ref, pea_ref, peb_ref, w1a_ref, w1b_ref, b1_ref, w2_ref, gain_ref, o_ref, *, norm, n_c):
    c = c_ref[0, 0]
    ua = _dot((c + pea_ref[...]).astype(BF16), w1a_ref[...])
    ub = _dot((c + peb_ref[...]).astype(BF16), w1b_ref[...])
    rows = c.shape[0]
    ub_next = pltpu.roll(ub, rows - 1, 0)
    hid = jax.nn.gelu(ua + ub_next + b1_ref[...])
    o = _dot(hid.astype(BF16), w2_ref[...])
    if norm:
        o = _row_rms(o, gain_ref[...])
    ridx = lax.broadcasted_iota(I32, o.shape, 0)
    o_ref[0, 0] = jnp.where(ridx < n_c, o, 0.0).astype(o_ref.dtype)


def _compress(c, pe, w1, b1, w2, gain, norm):
    b, g, rows, width = c.shape
    half = CMP_STRIDE * HEAD_DIM
    consts = [pe[:CMP_STRIDE].reshape(1, half), pe[CMP_STRIDE:].reshape(1, half),
              w1[:half].astype(BF16), w1[half:].astype(BF16), b1.reshape(1, -1), w2.astype(BF16),
              gain.reshape(1, -1)]
    return pl.pallas_call(
        functools.partial(_compress_kernel, norm=norm, n_c=rows - 1),
        grid=(b, g),
        in_specs=[pl.BlockSpec((1, 1, rows, width), lambda i, j: (i, j, 0, 0))] + [_const_spec(x.shape) for x in consts],
        out_specs=pl.BlockSpec((1, 1, rows, HEAD_DIM), lambda i, j: (i, j, 0, 0)),
        out_shape=jax.ShapeDtypeStruct((b, g, rows, HEAD_DIM), BF16),
        compiler_params=_params("parallel", "parallel"),
        name="compress_norm" if norm else "compress",
    )(c, *consts)


def _softmax_tile(slot, rows, ms, add, table, s_ref, p_ref):
    T = Q_BLOCK
    new_m, alphas = [], []
    for h in range(len(ms)):
        cols = slice(h * T, (h + 1) * T)
        s = s_ref[slot, :rows, cols]
        if add is not None:
            s = s + add
        if table is not None:
            s = s + table[:, cols]
        m_new = jnp.maximum(ms[h], jnp.max(s, axis=0, keepdims=True))
        alphas.append(jnp.exp2(ms[h] - m_new))
        new_m.append(m_new)
        p_ref[slot, :rows, cols] = jnp.exp2(s - m_new).astype(p_ref.dtype)
    return new_m, jnp.concatenate(alphas, axis=1)


def _attend_tiles(carry, q_ref, tiles, s_ref, p_ref):
    ms, acc = carry
    q_all = q_ref[0].reshape(len(ms) * Q_BLOCK, HEAD_DIM)
    rows = [t[0].shape[0] for t in tiles]
    s_ref[0, :rows[0]] = _dot_nt(tiles[0][0], q_all)
    alpha_prev = None
    for i, (_, _, add, table) in enumerate(tiles):
        slot = i % 2
        if i + 1 < len(tiles):
            s_ref[1 - slot, :rows[i + 1]] = _dot_nt(tiles[i + 1][0], q_all)
        ms, alpha = _softmax_tile(slot, rows[i], ms, add, table, s_ref, p_ref)
        if i > 0:
            acc = acc * alpha_prev + _dot(tiles[i - 1][1], p_ref[1 - slot, :rows[i - 1]])
        alpha_prev = alpha
    last = len(tiles) - 1
    return ms, acc * alpha_prev + _dot(tiles[last][1], p_ref[last % 2, :rows[last]])


def _attend_init(heads, rows=V_ROWS):
    return ([jnp.full((1, Q_BLOCK), NEG, F32) for _ in range(heads)],
            jnp.zeros((rows, heads * Q_BLOCK), F32))


def _attend_finish(carry):
    _, acc = carry
    return acc[:HEAD_DIM] / acc[HEAD_DIM:HEAD_DIM + 1]


def _far_attend(qi, heads, q_ref, load_k, load_vt, load_add, last_start, last_add, last_table, s_ref, p_ref):
    T, C = Q_BLOCK, KEY_CHUNK
    pairs = (qi // 2 + 1) // 2
    q_all = q_ref[0].reshape(heads * T, HEAD_DIM)
    tile = lambda j: pl.multiple_of(jnp.where(j >= 2 * pairs, last_start, jnp.maximum(j, 0) * C), T)
    logits = lambda j: _dot_nt(load_k(tile(j)), q_all)
    s_ref[0] = logits(0)
    p_ref[1] = jnp.zeros(p_ref.shape[1:], p_ref.dtype)

    softmax = lambda slot, j, ms: _softmax_tile(slot, C, ms, load_add(tile(j)), None, s_ref, p_ref)

    def body(i, carry):
        ms, acc, alpha_prev = carry
        a = 2 * i
        s_ref[1] = logits(a + 1)
        ms, alpha_a = softmax(0, a, ms)
        acc = acc * alpha_prev + _dot(load_vt(tile(a - 1)), p_ref[1])
        s_ref[0] = logits(a + 2)
        ms, alpha_b = softmax(1, a + 1, ms)
        acc = acc * alpha_a + _dot(load_vt(tile(a)), p_ref[0])
        return ms, acc, alpha_b

    ms, acc = _attend_init(heads)
    ms, acc, alpha_prev = lax.fori_loop(0, pairs, body, (ms, acc, jnp.ones((1, heads * T), F32)))
    acc = acc * alpha_prev + _dot(load_vt(tile(2 * pairs - 1)), p_ref[1])
    ms, alpha_last = _softmax_tile(0, C, ms, last_add, last_table, s_ref, p_ref)
    return ms, acc * alpha_last + _dot(load_vt(last_start), p_ref[0])


def _nsa_kernel(q_ref, kc_ref, vcx_ref, ks_ref, vst_ref, kw_ref, vwt_ref, gate_ref, band_ref, tdiag_ref,
                o_ref, sc_ref, sel_ref, s_ref, p_ref):
    qi = pl.program_id(2)
    hg, T, C = NSA_GROUP, Q_BLOCK, KEY_CHUNK
    t0 = qi * T
    n_cp = kc_ref.shape[2]
    n_s = sel_ref.shape[1]
    row_c = lax.broadcasted_iota(I32, (C, T), 0)
    tpos_c = t0 + lax.broadcasted_iota(I32, (C, T), 1)
    diag_start = pl.multiple_of(jnp.maximum(t0 - T, 0), T)

    band0 = pl.multiple_of(jnp.maximum(t0 // CMP_STRIDE - 2 * SUBLANES, 0), SUBLANES)
    sc_ref[...] = jnp.where(lax.broadcasted_iota(I32, sc_ref.shape, 0) >= band0 + CMP_BAND, NEG, 0.0)
    sc_ref[pl.ds(band0, CMP_BAND), :] = band_ref[0]
    rows_c = min(C, n_cp)
    cmp_tiles = [(kc_ref[0, 0, lo:lo + rows_c, :], vcx_ref[0, 0, :, lo:lo + rows_c], None,
                  sc_ref[lo:lo + rows_c, :]) for lo in range(0, n_cp, rows_c)]
    m_c, acc_c = _attend_tiles(_attend_init(hg, vcx_ref.shape[2]), q_ref, cmp_tiles, s_ref, p_ref)
    seen = jnp.concatenate(m_c, axis=1) > 0.5 * NEG
    acc_c = acc_c * jnp.where(seen, 1.0 / acc_c[HEAD_DIM:HEAD_DIM + 1], 0.0)
    o_c = [acc_c[:HEAD_DIM, h * T:(h + 1) * T] for h in range(hg)]
    imp = acc_c[V_ROWS:, 0:T]
    for h in range(1, hg):
        imp = imp + acc_c[V_ROWS:, h * T:(h + 1) * T]

    blk = lax.broadcasted_iota(I32, (n_s, T), 0)
    blk_f = blk.astype(F32)
    cur = (t0 + lax.broadcasted_iota(I32, (n_s, T), 1)) // SLC_LEN
    forced = (blk == 0) | (blk == cur) | (blk == cur - 1)
    work = jnp.where(forced, -3e38, jnp.where(blk <= cur, imp, -1e6))

    def pick(_, carry):
        work, sel = carry
        mx = jnp.max(work, axis=0, keepdims=True)
        first = jnp.min(jnp.where(work == mx, blk_f, float(n_s)), axis=0, keepdims=True)
        hit = blk_f == first
        return jnp.where(hit, -3e38, work), jnp.where(hit, 1.0, sel)

    _, sel = lax.fori_loop(0, max(min(SLC_TOPN, n_s) - N_FORCED, 0), pick,
                           (work, jnp.where(forced, 1.0, 0.0)))
    sel_add = jnp.where(sel > 0.5, 0.0, NEG)
    sel_ref[0] = sel_add
    sel_ref[1] = jnp.where(blk < (t0 - T) // SLC_LEN, sel_add, NEG)

    def block_mask(start, which):
        first = start // SLC_LEN
        return jnp.concatenate([jnp.broadcast_to(sel_ref[which, pl.ds(first + i, 1), :], (SLC_LEN, T))
                                for i in range(C // SLC_LEN)], axis=0)

    load_ks = lambda start: ks_ref[0, 0, pl.ds(start, C), :]
    load_vst = lambda start: vst_ref[0, 0, :, pl.ds(start, C)]
    o_s = _attend_finish(_far_attend(qi, hg, q_ref, load_ks, load_vst, lambda start: block_mask(start, 1),
                                     diag_start, block_mask(diag_start, 0), tdiag_ref[0], s_ref, p_ref))

    def win_tile(lo, rows, last):
        start = pl.multiple_of(jnp.maximum(lo, 0), T)
        pos = row_c[:rows] + start
        rel = tpos_c[:rows] - pos
        add = jnp.where((pos < lo + rows) & (rel >= 0) & (rel < WINDOW), 0.0, NEG)
        return (kw_ref[0, 0, pl.ds(start, rows), :], vwt_ref[0, 0, :, pl.ds(start, rows)], add,
                tdiag_ref[0] if last else None)

    win_tiles = [win_tile(t0 - T - 2 * C, C, False), win_tile(t0 - T - C, C, False), win_tile(t0 - T, C, True)]
    o_w = _attend_finish(_attend_tiles(_attend_init(hg), q_ref, win_tiles, s_ref, p_ref))

    gates = gate_ref[0, 0]
    for h in range(hg):
        cols = slice(h * T, (h + 1) * T)
        o = (gates[3 * h:3 * h + 1] * o_c[h] + gates[3 * h + 1:3 * h + 2] * o_s[:, cols]
             + gates[3 * h + 2:3 * h + 3] * o_w[:, cols])
        o_ref[0, :, h * HEAD_DIM:(h + 1) * HEAD_DIM] = o.T.astype(o_ref.dtype)


def _nsa(q, kc, vcx, ks, vst, kw, vwt, gates_t, band, tdiag):
    b, _, s, _ = q.shape
    g, hg = NSA_KV_HEADS, NSA_GROUP
    nq = s // Q_BLOCK
    n_cp = kc.shape[2]
    assert n_cp % min(KEY_CHUNK, n_cp) == 0
    rows = lambda r: pl.BlockSpec((1, 1, r, HEAD_DIM), lambda bi, gi, qi: (bi, gi, 0, 0))
    cols = lambda r, c: pl.BlockSpec((1, 1, r, c), lambda bi, gi, qi: (bi, gi, 0, 0))
    return pl.pallas_call(
        _nsa_kernel,
        grid=(b, g, nq),
        in_specs=[
            pl.BlockSpec((1, hg, Q_BLOCK, HEAD_DIM), lambda bi, gi, qi: (bi, gi, qi, 0)),
            rows(n_cp), cols(vcx.shape[2], n_cp), rows(s), cols(V_ROWS, s), rows(s), cols(V_ROWS, s),
            pl.BlockSpec((1, 1, hg * 3, Q_BLOCK), lambda bi, gi, qi: (bi, gi, 0, qi)),
            pl.BlockSpec((1, CMP_BAND, hg * Q_BLOCK), lambda bi, gi, qi: (jnp.minimum(qi, 2), 0, gi)),
            pl.BlockSpec((1, KEY_CHUNK, hg * Q_BLOCK), lambda bi, gi, qi: (jnp.minimum(qi, 1), 0, gi)),
        ],
        out_specs=pl.BlockSpec((1, Q_BLOCK, hg * HEAD_DIM), lambda bi, gi, qi: (bi, qi, gi)),
        out_shape=jax.ShapeDtypeStruct((b, s, NSA_HEADS * HEAD_DIM), BF16),
        scratch_shapes=[pltpu.VMEM((n_cp, hg * Q_BLOCK), F32), pltpu.VMEM((2, s // SLC_LEN, Q_BLOCK), F32),
                        pltpu.VMEM((2, KEY_CHUNK, hg * Q_BLOCK), F32), pltpu.VMEM((2, KEY_CHUNK, hg * Q_BLOCK), BF16)],
        compiler_params=_params("parallel", "parallel", "arbitrary"),
        name="nsa",
    )(q, kc, vcx, ks, vst, kw, vwt, gates_t, band, tdiag)


DIGIT_BITS = 14
DIGIT_BIAS = 0x80
DIGIT_DTYPE = jnp.bfloat16


def _pack_patterns(p):
    half = p.shape[0] // 2
    return (p[:half] << 16) | p[half:]


def _sortable_key(x):
    b = lax.bitcast_convert_type(jnp.where(x == 0.0, 0.0, x), I32)
    return b ^ ((b >> 31) & 0x7FFFFFFF)


def _dsa_kernel(q_ref, iq_ref, iw_ref, ik_ref, k_ref, vt_ref, tdiag_ref, o_ref, key_ref, dig_ref, thr_ref, s_ref, p_ref,
                *, n_keep):
    qi = pl.program_id(1)
    H, T, C = DSA_HEADS, Q_BLOCK, KEY_CHUNK
    t0 = qi * T
    n_chunks = qi // 2 + 1
    iw = iw_ref[0]
    row = lax.broadcasted_iota(I32, (C, T), 0)
    tpos = t0 + lax.broadcasted_iota(I32, (C, T), 1)
    int_min = jnp.iinfo(jnp.int32).min

    def score_step(j, _):
        for u in range(2):
            start = pl.multiple_of((2 * j + u) * C, C)
            ikc = ik_ref[0, pl.ds(start, C), :]
            score = jnp.maximum(_dot_nt(ikc, iq_ref[0, 0]), 0.0) * iw[0:1]
            for h in range(1, IDX_HEADS):
                score = score + jnp.maximum(_dot_nt(ikc, iq_ref[0, h]), 0.0) * iw[h:h + 1]
            score = jnp.where(row + start <= tpos, score, NEG)
            key = _sortable_key(score)
            key_ref[pl.ds(start, C), :] = key
            digit = lax.shift_right_logical(key ^ int_min, jnp.int32(32 - DIGIT_BITS))
            dig_ref[pl.ds(pl.multiple_of(start // 2, C // 2), C // 2), :] = _pack_patterns(digit + DIGIT_BIAS)
        return 0

    lax.fori_loop(0, (n_chunks + 1) // 2, score_step, 0)

    def count_digits(cand_digit):
        cand = cand_digit + DIGIT_BIAS
        cand = pltpu.bitcast(jnp.broadcast_to((cand << 16) | cand, (SUBLANES, T)), DIGIT_DTYPE)

        def tile_hits(j):
            words = dig_ref[pl.ds(pl.multiple_of(j * (C // 2), C // 2), C // 2), :]
            vals = pltpu.bitcast(words, DIGIT_DTYPE).reshape(C // (2 * SUBLANES), 2 * SUBLANES, T)
            hits = jnp.where(vals >= cand[None], jnp.ones((), DIGIT_DTYPE), jnp.zeros((), DIGIT_DTYPE))
            parts = [hits[i] for i in range(hits.shape[0])]
            while len(parts) > 1:
                parts = [parts[i] + parts[i + 1] for i in range(0, len(parts), 2)]
            return parts[0].astype(F32)

        def body(i, acc):
            second = 2 * i + 1
            keep = jnp.where(second < n_chunks, 1.0, 0.0)
            return acc + tile_hits(2 * i) + tile_hits(jnp.minimum(second, n_chunks - 1)) * keep

        acc = lax.fori_loop(0, (n_chunks + 1) // 2, body, jnp.zeros((2 * SUBLANES, T), F32))
        return jnp.sum(acc, axis=0, keepdims=True).astype(I32)

    def digit_search(need, n_all):
        def bit(i, carry):
            digit, n_at = carry
            cand = digit | lax.shift_left(jnp.int32(1), DIGIT_BITS - 1 - i)
            n_ge = count_digits(cand)
            ok = n_ge >= need
            return jnp.where(ok, cand, digit), jnp.where(ok, n_ge, n_at)

        return lax.fori_loop(0, DIGIT_BITS, bit, (jnp.zeros((1, T), I32), n_all))

    def count(pred_fn):
        def body(j, acc):
            start = pl.multiple_of(j * C, C)
            hits = jnp.where(pred_fn(key_ref[pl.ds(start, C), :], start), 1.0, 0.0)
            return acc + jnp.sum(hits.reshape(C // COUNT_ROWS, COUNT_ROWS, T), axis=0)
        acc = lax.fori_loop(0, n_chunks, body, jnp.zeros((COUNT_ROWS, T), F32))
        return jnp.sum(acc, axis=0, keepdims=True).astype(I32)

    d1, n_at1 = digit_search(n_keep, jnp.full((1, T), 1, I32) * (n_chunks * C))
    n_gt1 = count_digits(d1 + 1)

    def second_digits(j, _):
        start = pl.multiple_of(j * C, C)
        u = key_ref[pl.ds(start, C), :] ^ int_min
        digit = lax.shift_right_logical(u, jnp.int32(32 - 2 * DIGIT_BITS)) & (2 ** DIGIT_BITS - 1)
        first = lax.shift_right_logical(u, jnp.int32(32 - DIGIT_BITS))
        pattern = jnp.where(first == d1, digit + DIGIT_BIAS, 0)
        dig_ref[pl.ds(pl.multiple_of(start // 2, C // 2), C // 2), :] = _pack_patterns(pattern)
        return 0

    lax.fori_loop(0, n_chunks, second_digits, 0)
    d2, n_at2 = digit_search(n_keep - n_gt1, n_at1 - n_gt1)

    def thr_bit(i, carry):
        thr_u, n_at = carry
        cand_u = thr_u | lax.shift_left(jnp.int32(1), 31 - i)
        cand = cand_u ^ int_min
        n_ge = count(lambda keys, _: keys >= cand)
        ok = n_ge >= n_keep
        return jnp.where(ok, cand_u, thr_u), jnp.where(ok, n_ge, n_at)

    prefix = lax.shift_left(d1, jnp.int32(32 - DIGIT_BITS)) | lax.shift_left(d2, jnp.int32(32 - 2 * DIGIT_BITS))
    n_prefix = n_gt1 + n_at2
    thr_ref[...] = prefix

    @pl.when(jnp.max((n_prefix - n_keep).astype(F32)) > 0.0)
    def _():
        thr_u, n_at = lax.fori_loop(2 * DIGIT_BITS, 32, thr_bit, (prefix, n_prefix))
        thr_ref[...] = thr_u
        thr = thr_u ^ int_min

        @pl.when(jnp.max((n_at - n_keep).astype(F32)) > 0.0)
        def _():
            need = n_keep - count(lambda keys, _: keys > thr)
            n_pos_bits = int(math.ceil(math.log2(key_ref.shape[0] + 1)))

            def pos_bit(i, lim):
                cand = lim + lax.shift_left(jnp.int32(1), n_pos_bits - 1 - i)
                n_tied = count(lambda keys, start: (keys == thr) & (row + start < cand))
                return jnp.where(n_tied <= need, cand, lim)

            lim = lax.fori_loop(0, n_pos_bits, pos_bit, jnp.zeros((1, T), I32))

            def drop(j, _):
                start = pl.multiple_of(j * C, C)
                keys = key_ref[pl.ds(start, C), :]
                key_ref[pl.ds(start, C), :] = jnp.where((keys == thr) & (row + start >= lim), int_min, keys)
                return 0

            lax.fori_loop(0, n_chunks, drop, 0)

    thr = thr_ref[...] ^ int_min

    def key_mask(start, far):
        keep = key_ref[pl.ds(start, C), :] >= thr
        if far:
            keep = keep & (row + start < t0 - T)
        return jnp.where(keep, 0.0, NEG)

    load_k = lambda start: k_ref[0, pl.ds(start, C), :]
    load_vt = lambda start: vt_ref[0, :, pl.ds(start, C)]
    diag_start = pl.multiple_of(jnp.maximum(t0 - T, 0), T)
    o = _attend_finish(_far_attend(qi, H, q_ref, load_k, load_vt, lambda start: key_mask(start, True),
                                   diag_start, key_mask(diag_start, False), tdiag_ref[0], s_ref, p_ref))
    for h in range(H):
        o_ref[0, :, h * HEAD_DIM:(h + 1) * HEAD_DIM] = o[:, h * T:(h + 1) * T].T.astype(o_ref.dtype)


def _dsa(q, iq, iw_t, ik, k, vt, tdiag):
    b, _, s, _ = q.shape
    assert s % (2 * KEY_CHUNK) == 0
    nq = s // Q_BLOCK
    n_keep = min(DSA_TOPK_MAX, s // 4)
    heads = lambda: pl.BlockSpec((1, DSA_HEADS, Q_BLOCK, HEAD_DIM), lambda bi, qi: (bi, 0, qi, 0))
    rows = lambda: pl.BlockSpec((1, s, HEAD_DIM), lambda bi, qi: (bi, 0, 0))
    return pl.pallas_call(
        functools.partial(_dsa_kernel, n_keep=n_keep),
        grid=(b, nq),
        in_specs=[heads(), heads(),
                  pl.BlockSpec((1, IDX_HEADS, Q_BLOCK), lambda bi, qi: (bi, 0, qi)),
                  rows(), rows(),
                  pl.BlockSpec((1, V_ROWS, s), lambda bi, qi: (bi, 0, 0)),
                  pl.BlockSpec((1, KEY_CHUNK, DSA_HEADS * Q_BLOCK), lambda bi, qi: (jnp.minimum(qi, 1), 0, 1))],
        out_specs=pl.BlockSpec((1, Q_BLOCK, DSA_HEADS * HEAD_DIM), lambda bi, qi: (bi, qi, 0)),
        out_shape=jax.ShapeDtypeStruct((b, s, DSA_HEADS * HEAD_DIM), BF16),
        scratch_shapes=[pltpu.VMEM((s, Q_BLOCK), I32), pltpu.VMEM((s // 2, Q_BLOCK), I32),
                        pltpu.VMEM((1, Q_BLOCK), I32),
                        pltpu.VMEM((2, KEY_CHUNK, DSA_HEADS * Q_BLOCK), F32),
                        pltpu.VMEM((2, KEY_CHUNK, DSA_HEADS * Q_BLOCK), BF16)],
        compiler_params=_params("parallel", "arbitrary"),
        name="dsa",
    )(q, iq, iw_t, ik, k, vt, tdiag)


def _merge_kernel(x_ref, oa_ref, ob_ref, ga_ref, gb_ref, wa_ref, wb_ref, wo_ref, gn_ref, wr_ref, br_ref,
                  x1_ref, xn_ref, gate_ref, gate_t_ref):
    merged = ga_ref[...] * _dot(oa_ref[0], wa_ref[...]) + gb_ref[...] * _dot(ob_ref[0], wb_ref[...])
    x1 = x_ref[...] + _dot(merged.astype(BF16), wo_ref[...])
    x1_ref[...] = x1
    xn = _row_rms(x1, gn_ref[...]).astype(BF16)
    xn_ref[...] = xn

    logits = _dot(xn, wr_ref[...]) + br_ref[...]
    lane = lax.broadcasted_iota(I32, logits.shape, 1)
    work = jnp.where(lane < N_EXPERTS, logits, -3e38)
    gates = jnp.zeros_like(logits)
    denom = jnp.zeros((logits.shape[0], 1), F32)
    top = None
    for _ in range(MOE_TOPK):
        mx = jnp.max(work, axis=1, keepdims=True)
        first = jnp.min(jnp.where(work == mx, lane, LANES), axis=1, keepdims=True)
        hit = lane == first
        top = mx if top is None else top
        e = jnp.exp(mx - top)
        gates = jnp.where(hit, e, gates)
        denom = denom + e
        work = jnp.where(hit, -3e38, work)
    gates = gates / denom
    gate_ref[...] = gates
    gate_t_ref[...] = gates.T


def _merge(x, oa, ob, ga, gb, w_branch_a, w_branch_b, w_out, ffn_norm, w_router, b_router, tm):
    b, s, d = x.shape
    n = b * s
    xf = x.reshape(n, d)
    nt = s // tm
    wa = w_branch_a.astype(BF16)
    wb = w_branch_b.astype(BF16)
    wr = jnp.concatenate([w_router, jnp.zeros((d, LANES - N_EXPERTS), F32)], axis=1).astype(BF16)
    br = jnp.concatenate([b_router, jnp.zeros((LANES - N_EXPERTS,), F32)]).reshape(1, LANES)
    consts = [wa, wb, w_out.astype(BF16), ffn_norm.reshape(1, d), wr, br]
    row = lambda w: pl.BlockSpec((tm, w), lambda bi, ti: (bi * nt + ti, 0))
    heads = lambda: pl.BlockSpec((1, tm, NSA_HEADS * HEAD_DIM), lambda bi, ti: (bi, ti, 0))
    return pl.pallas_call(
        _merge_kernel,
        grid=(b, nt),
        in_specs=[row(d), heads(), heads(), row(d), row(d)] + [_const_spec(c.shape) for c in consts],
        out_specs=[row(d), row(d), row(LANES), pl.BlockSpec((LANES, tm), lambda bi, ti: (0, bi * nt + ti))],
        out_shape=[jax.ShapeDtypeStruct((n, d), F32), jax.ShapeDtypeStruct((n, d), BF16),
                   jax.ShapeDtypeStruct((n, LANES), F32), jax.ShapeDtypeStruct((LANES, n), F32)],
        compiler_params=_params("parallel", "parallel"),
        name="merge",
    )(xf, oa, ob, ga, gb, *consts)


def _deinterleave_index(width):
    half = MXU_DIM // 2
    c = np.arange(width)
    r = c % MXU_DIM
    return (c - r) + np.where(r < half, 2 * r, 2 * (r - half) + 1)


def _moe_prep_kernel(w_ref, perm_ref, o_ref):
    perm = perm_ref[...]
    for g in range(w_ref.shape[2] // MXU_DIM):
        cols = slice(g * MXU_DIM, (g + 1) * MXU_DIM)
        o_ref[0, :, cols] = _dot(w_ref[0, :, cols].astype(BF16), perm).astype(BF16)


def _moe_prep(w_gate_up, tr):
    e, d, f2 = w_gate_up.shape
    src = _deinterleave_index(MXU_DIM)
    perm = jnp.asarray((np.arange(MXU_DIM)[:, None] == src[None, :]).astype(np.float32), BF16)
    return pl.pallas_call(
        _moe_prep_kernel,
        grid=(e, d // tr),
        in_specs=[pl.BlockSpec((1, tr, f2), lambda i, j: (i, j, 0)), _const_spec(perm.shape)],
        out_specs=pl.BlockSpec((1, tr, f2), lambda i, j: (i, j, 0)),
        out_shape=jax.ShapeDtypeStruct((e, d, f2), BF16),
        compiler_params=_params("parallel", "parallel"),
        name="moe_prep",
    )(w_gate_up, perm)


def _moe_kernel(x1_ref, xn_ref, gate_ref, gate_t_ref, lower_ref, upper_ref, wgu_ref, bgu_ref, wd_ref, bd_ref,
                o_ref, rank_hi_ref, rank_lo_ref, rank_t_ref):
    e = pl.program_id(1)
    tm, R = xn_ref.shape[0], MOE_ROWS

    @pl.when(e == 0)
    def _():
        o_ref[...] = x1_ref[...]
        routed = jnp.where(gate_ref[...] > 0.0, 1.0, 0.0).astype(BF16)
        routed_t = jnp.where(gate_t_ref[...] > 0.0, 1.0, 0.0).astype(BF16)
        rank = jnp.where(gate_ref[...] > 0.0, _dot(lower_ref[...], routed), -1.0)
        hi = rank.astype(BF16)
        rank_hi_ref[...] = hi
        rank_lo_ref[...] = (rank - hi.astype(F32)).astype(BF16)
        rank_t_ref[...] = _dot(routed_t, upper_ref[...])

    g_row = gate_t_ref[pl.ds(e, 1), :]
    rank_row = jnp.where(g_row > 0.0, rank_t_ref[pl.ds(e, 1), :], -1.0)
    take_e = jnp.where(lax.broadcasted_iota(I32, (LANES, LANES), 0) == e, 1.0, 0.0).astype(BF16)
    rank_rep = _dot(rank_hi_ref[...], take_e) + _dot(rank_lo_ref[...], take_e)
    rank_cols = jnp.concatenate([rank_rep] * (R // LANES) + [rank_rep[:, :R % LANES]], axis=1)
    n_routed = jnp.sum(jnp.where(g_row > 0.0, 1.0, 0.0)).astype(I32)
    slot_r = lax.broadcasted_iota(I32, (R, tm), 0).astype(F32)
    slot_c = lax.broadcasted_iota(I32, (tm, R), 1).astype(F32)
    half = MXU_DIM // 2

    def block(j, _):
        base = (j * R).astype(F32)
        pick = jnp.where(rank_row - base == slot_r, 1.0, 0.0)
        xg = _dot(pick.astype(BF16), xn_ref[...]).astype(BF16)
        hu = _dot(xg, wgu_ref[0]) + bgu_ref[0]
        acts = []
        for g in range(hu.shape[1] // MXU_DIM):
            glu = jnp.minimum(hu[:, g * MXU_DIM:g * MXU_DIM + half], SWIGLU_LIMIT)
            lin = jnp.clip(hu[:, g * MXU_DIM + half:(g + 1) * MXU_DIM], -SWIGLU_LIMIT, SWIGLU_LIMIT)
            acts.append((glu * jax.nn.sigmoid(SWIGLU_ALPHA * glu) * (lin + 1.0)).astype(BF16))
        y = _dot(jnp.concatenate(acts, axis=1), wd_ref[0]) + bd_ref[0]
        g_r = jnp.sum(pick * g_row, axis=1, keepdims=True)
        put = jnp.where(rank_cols - base == slot_c, 1.0, 0.0).astype(BF16)
        o_ref[...] += _dot(put, (y * g_r).astype(BF16))
        return 0

    lax.fori_loop(0, (n_routed + R - 1) // R, block, 0)


def _moe(x1, xn, gates, gates_t, w_gate_up, b_gate_up, w_down, b_down, tm):
    n, d = x1.shape
    f = D_EXPERT
    wgu = _moe_prep(w_gate_up, tr=256)
    bgu = b_gate_up[:, _deinterleave_index(2 * f)].reshape(N_EXPERTS, 1, 2 * f)
    wd = w_down.astype(BF16)
    bd = b_down.reshape(N_EXPERTS, 1, d)
    earlier = np.arange(tm)[None, :] < np.arange(tm)[:, None]
    lower = jnp.asarray(earlier.astype(np.float32), BF16)
    upper = jnp.asarray(earlier.T.astype(np.float32), BF16)
    row = lambda w: pl.BlockSpec((tm, w), lambda i, e: (i, 0))
    per_e = lambda a, b_: pl.BlockSpec((1, a, b_), lambda i, e: (e, 0, 0))
    return pl.pallas_call(
        _moe_kernel,
        grid=(n // tm, N_EXPERTS),
        in_specs=[row(d), row(d), row(LANES), pl.BlockSpec((LANES, tm), lambda i, e: (0, i)),
                  _const_spec(lower.shape), _const_spec(upper.shape),
                  per_e(d, 2 * f), per_e(1, 2 * f), per_e(f, d), per_e(1, d)],
        out_specs=row(d),
        out_shape=jax.ShapeDtypeStruct((n, d), F32),
        scratch_shapes=[pltpu.VMEM((tm, LANES), BF16), pltpu.VMEM((tm, LANES), BF16), pltpu.VMEM((LANES, tm), F32)],
        compiler_params=_params("parallel", "arbitrary"),
        name="moe",
    )(x1, xn, gates, gates_t, lower, upper, wgu, bgu, wd, bd)


def _rel_bucket(rel):
    n = np.maximum(rel, 0)
    exact = REL_BUCKETS // 2
    nf = np.maximum(n, 1).astype(np.float64)
    large = exact + (np.log(nf / exact) / math.log(REL_MAX_DIST / exact) * (REL_BUCKETS - exact)).astype(np.int64)
    large = np.minimum(large, REL_BUCKETS - 1)
    return np.where(n < exact, n, large)


def _rel_table(rel_bias, rel):
    far = rel_bias[REL_BUCKETS - 1]
    one_hot = jax.nn.one_hot(jnp.asarray(_rel_bucket(rel).reshape(-1)), REL_BUCKETS, dtype=F32)
    picked = jnp.dot(one_hot, rel_bias, precision=lax.Precision.HIGHEST).reshape(rel.shape + (-1,))
    tab = jnp.where(jnp.asarray(rel)[..., None] >= 0, (picked - far) * LOG2E, NEG)
    tab = jnp.moveaxis(tab, -1, -2)
    return tab.reshape(tab.shape[:-2] + (-1,)).astype(F32)


def _bias_tables(rel_bias):
    i = np.arange(Q_BLOCK)[None, :]
    j = np.arange(KEY_CHUNK)[:, None]
    tdiag = _rel_table(rel_bias, np.stack([i - j, i - j + Q_BLOCK]))
    r = np.arange(CMP_BAND)[:, None]
    rel_band = np.stack([i - (CMP_LEN - 1) - CMP_STRIDE * (c0 + r) for c0 in (0, -SUBLANES, -2 * SUBLANES)])
    band = _rel_table(rel_bias[:, :NSA_HEADS], rel_band)
    return tdiag, band


def _overlap_matrix_t(s):
    n_cp = s // CMP_STRIDE
    n_s = s // SLC_LEN
    c_start = np.arange(n_cp) * CMP_STRIDE
    s_start = np.arange(n_s) * SLC_LEN
    ovl = (c_start[None, :] < s_start[:, None] + SLC_LEN) & (c_start[None, :] + CMP_LEN > s_start[:, None])
    ovl[:, n_cp - 1] = False
    return jnp.asarray(ovl.astype(np.float32), BF16)


def _layer(x, attn_norm, w_in, nsa_q_norm, nsa_k_norm, cmp_k_pe, cmp_k_w1, cmp_k_b1, cmp_k_w2,
           cmp_v_pe, cmp_v_w1, cmp_v_b1, cmp_v_w2, dsa_kv_norm, dsa_w_uk, dsa_w_uv, dsa_q_norm,
           dsa_k_norm, w_branch_a, w_branch_b, w_out, rel_bias, ffn_norm, w_router, b_router,
           w_gate_up, b_gate_up, w_down, b_down):
    b, s, d = x.shape
    n = b * s
    g = NSA_KV_HEADS
    (qa, ck, cv, ks, sv_t, kw, wv_t, misc_t, qb, kb, vb_t, iq, ik, ga, gb) = _in_proj(
        x, attn_norm, w_in, nsa_q_norm, nsa_k_norm, dsa_q_norm, dsa_kv_norm, dsa_w_uk, dsa_w_uv, dsa_k_norm,
        tm=min(256, s))

    chunks = lambda a: a.reshape(b, g, s // CMP_STRIDE, CMP_STRIDE * HEAD_DIM)
    kc = _compress(chunks(ck), cmp_k_pe, cmp_k_w1, cmp_k_b1, cmp_k_w2, nsa_k_norm, True)
    vc = _compress(chunks(cv), cmp_v_pe, cmp_v_w1, cmp_v_b1, cmp_v_w2, nsa_k_norm, False)

    tdiag, band = _bias_tables(rel_bias)
    gates_t = misc_t[:, :NSA_HEADS * 3].reshape(b, g, NSA_GROUP * 3, s)
    overlap_t = _overlap_matrix_t(s)
    vc_t = vc.transpose(0, 1, 3, 2)
    vcx = jnp.concatenate([vc_t, jnp.ones((b, g, V_ROWS - HEAD_DIM, vc_t.shape[3]), BF16),
                           jnp.broadcast_to(overlap_t, (b, g) + overlap_t.shape)], axis=2)
    oa = _nsa(qa, kc, vcx, ks, sv_t, kw, wv_t, gates_t, band, tdiag)

    iw_t = misc_t[:, NSA_HEADS * 3:NSA_HEADS * 3 + IDX_HEADS]
    ob = _dsa(qb, iq, iw_t, ik.reshape(b, s, IDX_DIM), kb.reshape(b, s, HEAD_DIM), vb_t, tdiag)

    x1, xn, moe_gates, moe_gates_t = _merge(x, oa, ob, ga, gb, w_branch_a, w_branch_b, w_out, ffn_norm, w_router,
                               b_router, tm=min(512, s))
    out = _moe(x1, xn, moe_gates, moe_gates_t, w_gate_up, b_gate_up, w_down, b_down, tm=min(MOE_TOKENS, n))
    return out.reshape(b, s, d)


def kernel(x, attn_norm, w_in, nsa_q_norm, nsa_k_norm, cmp_k_pe, cmp_k_w1, cmp_k_b1, cmp_k_w2, cmp_v_pe, cmp_v_w1, cmp_v_b1, cmp_v_w2, dsa_kv_norm, dsa_w_uk, dsa_w_uv, dsa_q_norm, dsa_k_norm, w_branch_a, w_branch_b, w_out, rel_bias, ffn_norm, w_router, b_router, w_gate_up, b_gate_up, w_down, b_down):
    for l in range(attn_norm.shape[0]):
        x = _layer(x, attn_norm[l], w_in[l], nsa_q_norm[l], nsa_k_norm[l], cmp_k_pe[l], cmp_k_w1[l],
                   cmp_k_b1[l], cmp_k_w2[l], cmp_v_pe[l], cmp_v_w1[l], cmp_v_b1[l], cmp_v_w2[l],
                   dsa_kv_norm[l], dsa_w_uk[l], dsa_w_uv[l], dsa_q_norm[l], dsa_k_norm[l],
                   w_branch_a[l], w_branch_b[l], w_out[l], rel_bias, ffn_norm[l], w_router[l],
                   b_router[l], w_gate_up[l], b_gate_up[l], w_down[l], b_down[l])
    return x
```

```python
import functools
import math

import numpy as np
import jax
import jax.numpy as jnp
from jax import lax
from jax.experimental import pallas as pl
from jax.experimental.pallas import tpu as pltpu

D_MODEL = 1024
HEAD_DIM = 64
NSA_HEADS = 8
NSA_KV_HEADS = 2
NSA_GROUP = NSA_HEADS // NSA_KV_HEADS
CMP_LEN = 32
CMP_STRIDE = 16
CMP_HID = 128
SLC_LEN = 64
SLC_TOPN = 16
WINDOW = 512
DSA_HEADS = 8
DSA_KV_RANK = 128
IDX_HEADS = 8
IDX_DIM = 64
DSA_TOPK_MAX = 256
N_EXPERTS = 32
MOE_TOPK = 4
D_EXPERT = 1024
SWIGLU_LIMIT = 7.0
SWIGLU_ALPHA = 1.702
REL_BUCKETS = 32
REL_MAX_DIST = 128
NORM_EPS = 1e-6
NEG = -1e30
LOG2E = math.log2(math.e)

Q_BLOCK = 128
KEY_CHUNK = 256
N_FORCED = 3
CMP_BAND = 24
COUNT_ROWS = 32
LANES = 128
SUBLANES = 8
MXU_DIM = 256
V_ROWS = HEAD_DIM + SUBLANES
MOE_TOKENS = 1024
MOE_ROWS = 160
VMEM_LIMIT = 56 * 1024 * 1024

F32 = jnp.float32
BF16 = jnp.bfloat16
I32 = jnp.int32


def _params(*sem):
    return pltpu.CompilerParams(dimension_semantics=sem, vmem_limit_bytes=VMEM_LIMIT)


def _dot(a, b):
    return jnp.dot(a, b, preferred_element_type=F32)


def _dot_nt(a, b):
    return lax.dot_general(a, b, (((1,), (1,)), ((), ())), preferred_element_type=F32)


def _const_spec(shape):
    nd = len(shape)
    return pl.BlockSpec(shape, lambda *_: (0,) * nd, pipeline_mode=pl.Buffered(1))


def _segment_sumsq(y, seg_ones):
    y2 = y * y
    hi = y2.astype(BF16)
    lo = (y2 - hi.astype(F32)).astype(BF16)
    return _dot(hi, seg_ones) + _dot(lo, seg_ones)


def _head_rms(y, seg_ones, gain):
    ms = _segment_sumsq(y, seg_ones) * (1.0 / HEAD_DIM)
    return y * lax.rsqrt(ms + NORM_EPS) * gain


def _row_rms(y, gain):
    ms = jnp.mean(y * y, axis=-1, keepdims=True)
    return y * lax.rsqrt(ms + NORM_EPS) * gain


def _in_proj_kernel(x_ref, g_ref, wqa_ref, wkv_ref, wmisc_ref, wqb_ref, wckv_ref, wiq_ref, wik_ref,
                    wga_ref, wgb_ref, seg512_ref, seg128_ref, gqa_ref, gka_ref, gqb_ref, gckv_ref,
                    wuk_ref, wuv_ref, gkb_ref,
                    qa_ref, ck_ref, cv_ref, ks_ref, sv_ref, kw_ref, wv_ref, misc_ref, qb_ref,
                    kb_ref, vb_ref, iq_ref, ik_ref, ga_ref, gb_ref):
    h = _row_rms(x_ref[...], g_ref[...]).astype(BF16)
    scale = HEAD_DIM ** -0.5 * LOG2E
    seg512 = seg512_ref[...]
    seg128 = seg128_ref[...]

    def put_heads(o_ref, y):
        for i in range(o_ref.shape[1]):
            o_ref[0, i] = y[:, i * HEAD_DIM:(i + 1) * HEAD_DIM].astype(o_ref.dtype)

    def put_values_t(o_ref, y):
        y_t = y.T
        ones = jnp.ones((V_ROWS - HEAD_DIM, y.shape[0]), o_ref.dtype)
        for i in range(o_ref.shape[1]):
            o_ref[0, i, :HEAD_DIM, :] = y_t[i * HEAD_DIM:(i + 1) * HEAD_DIM].astype(o_ref.dtype)
            o_ref[0, i, HEAD_DIM:, :] = ones

    qa = _head_rms(_dot(h, wqa_ref[...]), seg512, gqa_ref[...])
    put_heads(qa_ref, qa * scale)

    kv = _dot(h, wkv_ref[...])
    put_heads(ck_ref, kv[:, 0:128])
    put_heads(cv_ref, kv[:, 128:256])
    put_heads(ks_ref, _head_rms(kv[:, 256:384], seg128, gka_ref[...]))
    put_values_t(sv_ref, kv[:, 384:512])
    put_heads(kw_ref, _head_rms(kv[:, 512:640], seg128, gka_ref[...]))
    put_values_t(wv_ref, kv[:, 640:768])

    misc = _dot(h, wmisc_ref[...])
    lane = lax.broadcasted_iota(I32, misc.shape, 1)
    iw_scale = (IDX_HEADS ** -0.5)
    idx_scale = IDX_DIM ** -0.5
    misc = jnp.where(lane < NSA_HEADS * 3, jax.nn.sigmoid(misc), (misc * iw_scale) * idx_scale)
    misc_ref[0] = misc.T[:misc_ref.shape[1]]

    qb = _head_rms(_dot(h, wqb_ref[...]), seg512, gqb_ref[...])
    put_heads(qb_ref, qb * scale)

    ckv = _row_rms(_dot(h, wckv_ref[...]), gckv_ref[...]).astype(BF16)
    kb_ref[...] = _row_rms(_dot(ckv, wuk_ref[...]), gkb_ref[...]).astype(BF16)
    vb = _dot(ckv, wuv_ref[...])
    vb_ref[0, :HEAD_DIM, :] = vb.T.astype(BF16)
    vb_ref[0, HEAD_DIM:, :] = jnp.ones((V_ROWS - HEAD_DIM, vb.shape[0]), BF16)

    put_heads(iq_ref, _dot(h, wiq_ref[...]))
    ik_ref[...] = _dot(h, wik_ref[...]).astype(BF16)
    ga_ref[...] = jax.nn.sigmoid(_dot(h, wga_ref[...]))
    gb_ref[...] = jax.nn.sigmoid(_dot(h, wgb_ref[...]))


def _in_proj(x, attn_norm, w_in, nsa_q_norm, nsa_k_norm, dsa_q_norm, dsa_kv_norm, dsa_w_uk, dsa_w_uv,
             dsa_k_norm, tm):
    b, s, _ = x.shape
    n = b * s
    xf = x.reshape(n, -1)
    kvw = NSA_KV_HEADS * HEAD_DIM
    widths = (NSA_HEADS * HEAD_DIM, kvw, kvw, kvw, kvw, kvw, kvw, NSA_HEADS * 3,
              DSA_HEADS * HEAD_DIM, DSA_KV_RANK, IDX_HEADS * IDX_DIM, IDX_DIM, IDX_HEADS, D_MODEL, D_MODEL)
    offs = np.concatenate([[0], np.cumsum(widths)])
    sec = [w_in[:, offs[i]:offs[i + 1]] for i in range(len(widths))]
    wb = lambda w: w.astype(BF16)
    wqa = wb(sec[0])
    wkv = wb(jnp.concatenate(sec[1:7], axis=1))
    wmisc = wb(jnp.concatenate([sec[7], sec[12], jnp.zeros((D_MODEL, LANES - 32), F32)], axis=1))
    wqb, wckv, wiq, wik, wga, wgb = wb(sec[8]), wb(sec[9]), wb(sec[10]), wb(sec[11]), wb(sec[13]), wb(sec[14])

    def seg_ones(w):
        i = np.arange(w) // HEAD_DIM
        return jnp.asarray((i[:, None] == i[None, :]).astype(np.float32), BF16)

    consts = [attn_norm.reshape(1, -1), wqa, wkv, wmisc, wqb, wckv, wiq, wik, wga, wgb,
              seg_ones(512), seg_ones(128),
              jnp.tile(nsa_q_norm, NSA_HEADS).reshape(1, -1), jnp.tile(nsa_k_norm, NSA_KV_HEADS).reshape(1, -1),
              jnp.tile(dsa_q_norm, DSA_HEADS).reshape(1, -1), dsa_kv_norm.reshape(1, -1),
              wb(dsa_w_uk), wb(dsa_w_uv), dsa_k_norm.reshape(1, -1)]
    nt = s // tm
    g = NSA_KV_HEADS
    rows = lambda w, dt: (jax.ShapeDtypeStruct((n, w), dt), pl.BlockSpec((tm, w), lambda bi, ti: (bi * nt + ti, 0)))
    heads = lambda hh, dt=BF16: (jax.ShapeDtypeStruct((b, hh, s, HEAD_DIM), dt),
                                 pl.BlockSpec((1, hh, tm, HEAD_DIM), lambda bi, ti: (bi, 0, ti, 0)))
    values_t = (jax.ShapeDtypeStruct((b, g, V_ROWS, s), BF16),
                pl.BlockSpec((1, g, V_ROWS, tm), lambda bi, ti: (bi, 0, 0, ti)))
    cols = lambda r, dt: (jax.ShapeDtypeStruct((b, r, s), dt), pl.BlockSpec((1, r, tm), lambda bi, ti: (bi, 0, ti)))
    outs = [heads(NSA_HEADS), heads(g, F32), heads(g, F32), heads(g), values_t, heads(g), values_t,
            cols(32, F32), heads(DSA_HEADS), rows(HEAD_DIM, BF16), cols(V_ROWS, BF16), heads(IDX_HEADS),
            rows(IDX_DIM, BF16), rows(D_MODEL, F32), rows(D_MODEL, F32)]
    return pl.pallas_call(
        _in_proj_kernel,
        grid=(b, nt),
        in_specs=[rows(D_MODEL, F32)[1]] + [_const_spec(c.shape) for c in consts],
        out_specs=[spec for _, spec in outs],
        out_shape=[shape for shape, _ in outs],
        compiler_params=_params("parallel", "parallel"),
        name="in_proj",
    )(xf, *consts)


def _compress_kernel(c_ref, pea_ref, peb_ref, w1a_ref, w1b_ref, b1_ref, w2_ref, gain_ref, o_ref, *, norm, n_c):
    c = c_ref[0, 0]
    ua = _dot((c + pea_ref[...]).astype(BF16), w1a_ref[...])
    ub = _dot((c + peb_ref[...]).astype(BF16), w1b_ref[...])
    rows = c.shape[0]
    ub_next = pltpu.roll(ub, rows - 1, 0)
    hid = jax.nn.gelu(ua + ub_next + b1_ref[...])
    o = _dot(hid.astype(BF16), w2_ref[...])
    if norm:
        o = _row_rms(o, gain_ref[...])
    ridx = lax.broadcasted_iota(I32, o.shape, 0)
    o_ref[0, 0] = jnp.where(ridx < n_c, o, 0.0).astype(o_ref.dtype)


def _compress(c, pe, w1, b1, w2, gain, norm):
    b, g, rows, width = c.shape
    half = CMP_STRIDE * HEAD_DIM
    consts = [pe[:CMP_STRIDE].reshape(1, half), pe[CMP_STRIDE:].reshape(1, half),
              w1[:half].astype(BF16), w1[half:].astype(BF16), b1.reshape(1, -1), w2.astype(BF16),
              gain.reshape(1, -1)]
    return pl.pallas_call(
        functools.partial(_compress_kernel, norm=norm, n_c=rows - 1),
        grid=(b, g),
        in_specs=[pl.BlockSpec((1, 1, rows, width), lambda i, j: (i, j, 0, 0))] + [_const_spec(x.shape) for x in consts],
        out_specs=pl.BlockSpec((1, 1, rows, HEAD_DIM), lambda i, j: (i, j, 0, 0)),
        out_shape=jax.ShapeDtypeStruct((b, g, rows, HEAD_DIM), BF16),
        compiler_params=_params("parallel", "parallel"),
        name="compress_norm" if norm else "compress",
    )(c, *consts)


def _softmax_tile(slot, rows, ms, add, table, s_ref, p_ref):
    T = Q_BLOCK
    new_m, alphas = [], []
    for h in range(len(ms)):
        cols = slice(h * T, (h + 1) * T)
        s = s_ref[slot, :rows, cols]
        if add is not None:
            s = s + add
        if table is not None:
            s = s + table[:, cols]
        m_new = jnp.maximum(ms[h], jnp.max(s, axis=0, keepdims=True))
        alphas.append(jnp.exp2(ms[h] - m_new))
        new_m.append(m_new)
        p_ref[slot, :rows, cols] = jnp.exp2(s - m_new).astype(p_ref.dtype)
    return new_m, jnp.concatenate(alphas, axis=1)


def _attend_tiles(carry, q_ref, tiles, s_ref, p_ref):
    ms, acc = carry
    q_all = q_ref[0].reshape(len(ms) * Q_BLOCK, HEAD_DIM)
    rows = [t[0].shape[0] for t in tiles]
    s_ref[0, :rows[0]] = _dot_nt(tiles[0][0], q_all)
    alpha_prev = None
    for i, (_, _, add, table) in enumerate(tiles):
        slot = i % 2
        if i + 1 < len(tiles):
            s_ref[1 - slot, :rows[i + 1]] = _dot_nt(tiles[i + 1][0], q_all)
        ms, alpha = _softmax_tile(slot, rows[i], ms, add, table, s_ref, p_ref)
        if i > 0:
            acc = acc * alpha_prev + _dot(tiles[i - 1][1], p_ref[1 - slot, :rows[i - 1]])
        alpha_prev = alpha
    last = len(tiles) - 1
    return ms, acc * alpha_prev + _dot(tiles[last][1], p_ref[last % 2, :rows[last]])


def _attend_init(heads, rows=V_ROWS):
    return ([jnp.full((1, Q_BLOCK), NEG, F32) for _ in range(heads)],
            jnp.zeros((rows, heads * Q_BLOCK), F32))


def _attend_finish(carry):
    _, acc = carry
    return acc[:HEAD_DIM] / acc[HEAD_DIM:HEAD_DIM + 1]


def _far_attend(qi, heads, q_ref, load_k, load_vt, load_add, last_start, last_add, last_table, s_ref, p_ref):
    T, C = Q_BLOCK, KEY_CHUNK
    pairs = (qi // 2 + 1) // 2
    q_all = q_ref[0].reshape(heads * T, HEAD_DIM)
    tile = lambda j: pl.multiple_of(jnp.where(j >= 2 * pairs, last_start, jnp.maximum(j, 0) * C), T)
    logits = lambda j: _dot_nt(load_k(tile(j)), q_all)
    s_ref[0] = logits(0)
    p_ref[1] = jnp.zeros(p_ref.shape[1:], p_ref.dtype)

    softmax = lambda slot, j, ms: _softmax_tile(slot, C, ms, load_add(tile(j)), None, s_ref, p_ref)

    def body(i, carry):
        ms, acc, alpha_prev = carry
        a = 2 * i
        s_ref[1] = logits(a + 1)
        ms, alpha_a = softmax(0, a, ms)
        acc = acc * alpha_prev + _dot(load_vt(tile(a - 1)), p_ref[1])
        s_ref[0] = logits(a + 2)
        ms, alpha_b = softmax(1, a + 1, ms)
        acc = acc * alpha_a + _dot(load_vt(tile(a)), p_ref[0])
        return ms, acc, alpha_b

    ms, acc = _attend_init(heads)
    ms, acc, alpha_prev = lax.fori_loop(0, pairs, body, (ms, acc, jnp.ones((1, heads * T), F32)))
    acc = acc * alpha_prev + _dot(load_vt(tile(2 * pairs - 1)), p_ref[1])
    ms, alpha_last = _softmax_tile(0, C, ms, last_add, last_table, s_ref, p_ref)
    return ms, acc * alpha_last + _dot(load_vt(last_start), p_ref[0])


def _nsa_kernel(q_ref, kc_ref, vcx_ref, ks_ref, vst_ref, kw_ref, vwt_ref, gate_ref, band_ref, tdiag_ref,
                o_ref, sc_ref, sel_ref, s_ref, p_ref):
    qi = pl.program_id(2)
    hg, T, C = NSA_GROUP, Q_BLOCK, KEY_CHUNK
    t0 = qi * T
    n_cp = kc_ref.shape[2]
    n_s = sel_ref.shape[1]
    row_c = lax.broadcasted_iota(I32, (C, T), 0)
    tpos_c = t0 + lax.broadcasted_iota(I32, (C, T), 1)
    diag_start = pl.multiple_of(jnp.maximum(t0 - T, 0), T)

    band0 = pl.multiple_of(jnp.maximum(t0 // CMP_STRIDE - 2 * SUBLANES, 0), SUBLANES)
    sc_ref[...] = jnp.where(lax.broadcasted_iota(I32, sc_ref.shape, 0) >= band0 + CMP_BAND, NEG, 0.0)
    sc_ref[pl.ds(band0, CMP_BAND), :] = band_ref[0]
    rows_c = min(C, n_cp)
    cmp_tiles = [(kc_ref[0, 0, lo:lo + rows_c, :], vcx_ref[0, 0, :, lo:lo + rows_c], None,
                  sc_ref[lo:lo + rows_c, :]) for lo in range(0, n_cp, rows_c)]
    m_c, acc_c = _attend_tiles(_attend_init(hg, vcx_ref.shape[2]), q_ref, cmp_tiles, s_ref, p_ref)
    seen = jnp.concatenate(m_c, axis=1) > 0.5 * NEG
    acc_c = acc_c * jnp.where(seen, 1.0 / acc_c[HEAD_DIM:HEAD_DIM + 1], 0.0)
    o_c = [acc_c[:HEAD_DIM, h * T:(h + 1) * T] for h in range(hg)]
    imp = acc_c[V_ROWS:, 0:T]
    for h in range(1, hg):
        imp = imp + acc_c[V_ROWS:, h * T:(h + 1) * T]

    blk = lax.broadcasted_iota(I32, (n_s, T), 0)
    blk_f = blk.astype(F32)
    cur = (t0 + lax.broadcasted_iota(I32, (n_s, T), 1)) // SLC_LEN
    forced = (blk == 0) | (blk == cur) | (blk == cur - 1)
    work = jnp.where(forced, -3e38, jnp.where(blk <= cur, imp, -1e6))

    def pick(_, carry):
        work, sel = carry
        mx = jnp.max(work, axis=0, keepdims=True)
        first = jnp.min(jnp.where(work == mx, blk_f, float(n_s)), axis=0, keepdims=True)
        hit = blk_f == first
        return jnp.where(hit, -3e38, work), jnp.where(hit, 1.0, sel)

    _, sel = lax.fori_loop(0, max(min(SLC_TOPN, n_s) - N_FORCED, 0), pick,
                           (work, jnp.where(forced, 1.0, 0.0)))
    sel_add = jnp.where(sel > 0.5, 0.0, NEG)
    sel_ref[0] = sel_add
    sel_ref[1] = jnp.where(blk < (t0 - T) // SLC_LEN, sel_add, NEG)

    def block_mask(start, which):
        first = start // SLC_LEN
        return jnp.concatenate([jnp.broadcast_to(sel_ref[which, pl.ds(first + i, 1), :], (SLC_LEN, T))
                                for i in range(C // SLC_LEN)], axis=0)

    load_ks = lambda start: ks_ref[0, 0, pl.ds(start, C), :]
    load_vst = lambda start: vst_ref[0, 0, :, pl.ds(start, C)]
    o_s = _attend_finish(_far_attend(qi, hg, q_ref, load_ks, load_vst, lambda start: block_mask(start, 1),
                                     diag_start, block_mask(diag_start, 0), tdiag_ref[0], s_ref, p_ref))

    def win_tile(lo, rows, last):
        start = pl.multiple_of(jnp.maximum(lo, 0), T)
        pos = row_c[:rows] + start
        rel = tpos_c[:rows] - pos
        add = jnp.where((pos < lo + rows) & (rel >= 0) & (rel < WINDOW), 0.0, NEG)
        return (kw_ref[0, 0, pl.ds(start, rows), :], vwt_ref[0, 0, :, pl.ds(start, rows)], add,
                tdiag_ref[0] if last else None)

    win_tiles = [win_tile(t0 - T - 2 * C, C, False), win_tile(t0 - T - C, C, False), win_tile(t0 - T, C, True)]
    o_w = _attend_finish(_attend_tiles(_attend_init(hg), q_ref, win_tiles, s_ref, p_ref))

    gates = gate_ref[0, 0]
    for h in range(hg):
        cols = slice(h * T, (h + 1) * T)
        o = (gates[3 * h:3 * h + 1] * o_c[h] + gates[3 * h + 1:3 * h + 2] * o_s[:, cols]
             + gates[3 * h + 2:3 * h + 3] * o_w[:, cols])
        o_ref[0, :, h * HEAD_DIM:(h + 1) * HEAD_DIM] = o.T.astype(o_ref.dtype)


def _nsa(q, kc, vcx, ks, vst, kw, vwt, gates_t, band, tdiag):
    b, _, s, _ = q.shape
    g, hg = NSA_KV_HEADS, NSA_GROUP
    nq = s // Q_BLOCK
    n_cp = kc.shape[2]
    assert n_cp % min(KEY_CHUNK, n_cp) == 0
    rows = lambda r: pl.BlockSpec((1, 1, r, HEAD_DIM), lambda bi, gi, qi: (bi, gi, 0, 0))
    cols = lambda r, c: pl.BlockSpec((1, 1, r, c), lambda bi, gi, qi: (bi, gi, 0, 0))
    return pl.pallas_call(
        _nsa_kernel,
        grid=(b, g, nq),
        in_specs=[
            pl.BlockSpec((1, hg, Q_BLOCK, HEAD_DIM), lambda bi, gi, qi: (bi, gi, qi, 0)),
            rows(n_cp), cols(vcx.shape[2], n_cp), rows(s), cols(V_ROWS, s), rows(s), cols(V_ROWS, s),
            pl.BlockSpec((1, 1, hg * 3, Q_BLOCK), lambda bi, gi, qi: (bi, gi, 0, qi)),
            pl.BlockSpec((1, CMP_BAND, hg * Q_BLOCK), lambda bi, gi, qi: (jnp.minimum(qi, 2), 0, gi)),
            pl.BlockSpec((1, KEY_CHUNK, hg * Q_BLOCK), lambda bi, gi, qi: (jnp.minimum(qi, 1), 0, gi)),
        ],
        out_specs=pl.BlockSpec((1, Q_BLOCK, hg * HEAD_DIM), lambda bi, gi, qi: (bi, qi, gi)),
        out_shape=jax.ShapeDtypeStruct((b, s, NSA_HEADS * HEAD_DIM), BF16),
        scratch_shapes=[pltpu.VMEM((n_cp, hg * Q_BLOCK), F32), pltpu.VMEM((2, s // SLC_LEN, Q_BLOCK), F32),
                        pltpu.VMEM((2, KEY_CHUNK, hg * Q_BLOCK), F32), pltpu.VMEM((2, KEY_CHUNK, hg * Q_BLOCK), BF16)],
        compiler_params=_params("parallel", "parallel", "arbitrary"),
        name="nsa",
    )(q, kc, vcx, ks, vst, kw, vwt, gates_t, band, tdiag)


DIGIT_BITS = 14
DIGIT_BIAS = 0x80
DIGIT_DTYPE = jnp.bfloat16


def _pack_patterns(p):
    half = p.shape[0] // 2
    return (p[:half] << 16) | p[half:]


def _sortable_key(x):
    b = lax.bitcast_convert_type(jnp.where(x == 0.0, 0.0, x), I32)
    return b ^ ((b >> 31) & 0x7FFFFFFF)


def _dsa_kernel(q_ref, iq_ref, iw_ref, ik_ref, k_ref, vt_ref, tdiag_ref, o_ref, key_ref, dig_ref, thr_ref, s_ref, p_ref,
                *, n_keep):
    qi = pl.program_id(1)
    H, T, C = DSA_HEADS, Q_BLOCK, KEY_CHUNK
    t0 = qi * T
    n_chunks = qi // 2 + 1
    iw = iw_ref[0]
    row = lax.broadcasted_iota(I32, (C, T), 0)
    tpos = t0 + lax.broadcasted_iota(I32, (C, T), 1)
    int_min = jnp.iinfo(jnp.int32).min

    def score_step(j, _):
        for u in range(2):
            start = pl.multiple_of((2 * j + u) * C, C)
            ikc = ik_ref[0, pl.ds(start, C), :]
            score = jnp.maximum(_dot_nt(ikc, iq_ref[0, 0]), 0.0) * iw[0:1]
            for h in range(1, IDX_HEADS):
                score = score + jnp.maximum(_dot_nt(ikc, iq_ref[0, h]), 0.0) * iw[h:h + 1]
            score = jnp.where(row + start <= tpos, score, NEG)
            key = _sortable_key(score)
            key_ref[pl.ds(start, C), :] = key
            digit = lax.shift_right_logical(key ^ int_min, jnp.int32(32 - DIGIT_BITS))
            dig_ref[pl.ds(pl.multiple_of(start // 2, C // 2), C // 2), :] = _pack_patterns(digit + DIGIT_BIAS)
        return 0

    lax.fori_loop(0, (n_chunks + 1) // 2, score_step, 0)

    def count_digits(cand_digit):
        cand = cand_digit + DIGIT_BIAS
        cand = pltpu.bitcast(jnp.broadcast_to((cand << 16) | cand, (SUBLANES, T)), DIGIT_DTYPE)

        def tile_hits(j):
            words = dig_ref[pl.ds(pl.multiple_of(j * (C // 2), C // 2), C // 2), :]
            vals = pltpu.bitcast(words, DIGIT_DTYPE).reshape(C // (2 * SUBLANES), 2 * SUBLANES, T)
            hits = jnp.where(vals >= cand[None], jnp.ones((), DIGIT_DTYPE), jnp.zeros((), DIGIT_DTYPE))
            parts = [hits[i] for i in range(hits.shape[0])]
            while len(parts) > 1:
                parts = [parts[i] + parts[i + 1] for i in range(0, len(parts), 2)]
            return parts[0].astype(F32)

        def body(i, acc):
            second = 2 * i + 1
            keep = jnp.where(second < n_chunks, 1.0, 0.0)
            return acc + tile_hits(2 * i) + tile_hits(jnp.minimum(second, n_chunks - 1)) * keep

        acc = lax.fori_loop(0, (n_chunks + 1) // 2, body, jnp.zeros((2 * SUBLANES, T), F32))
        return jnp.sum(acc, axis=0, keepdims=True).astype(I32)

    def digit_search(need, n_all):
        def bit(i, carry):
            digit, n_at = carry
            cand = digit | lax.shift_left(jnp.int32(1), DIGIT_BITS - 1 - i)
            n_ge = count_digits(cand)
            ok = n_ge >= need
            return jnp.where(ok, cand, digit), jnp.where(ok, n_ge, n_at)

        return lax.fori_loop(0, DIGIT_BITS, bit, (jnp.zeros((1, T), I32), n_all))

    def count(pred_fn):
        def body(j, acc):
            start = pl.multiple_of(j * C, C)
            hits = jnp.where(pred_fn(key_ref[pl.ds(start, C), :], start), 1.0, 0.0)
            return acc + jnp.sum(hits.reshape(C // COUNT_ROWS, COUNT_ROWS, T), axis=0)
        acc = lax.fori_loop(0, n_chunks, body, jnp.zeros((COUNT_ROWS, T), F32))
        return jnp.sum(acc, axis=0, keepdims=True).astype(I32)

    d1, n_at1 = digit_search(n_keep, jnp.full((1, T), 1, I32) * (n_chunks * C))
    n_gt1 = count_digits(d1 + 1)

    def second_digits(j, _):
        start = pl.multiple_of(j * C, C)
        u = key_ref[pl.ds(start, C), :] ^ int_min
        digit = lax.shift_right_logical(u, jnp.int32(32 - 2 * DIGIT_BITS)) & (2 ** DIGIT_BITS - 1)
        first = lax.shift_right_logical(u, jnp.int32(32 - DIGIT_BITS))
        pattern = jnp.where(first == d1, digit + DIGIT_BIAS, 0)
        dig_ref[pl.ds(pl.multiple_of(start // 2, C // 2), C // 2), :] = _pack_patterns(pattern)
        return 0

    lax.fori_loop(0, n_chunks, second_digits, 0)
    d2, n_at2 = digit_search(n_keep - n_gt1, n_at1 - n_gt1)

    def thr_bit(i, carry):
        thr_u, n_at = carry
        cand_u = thr_u | lax.shift_left(jnp.int32(1), 31 - i)
        cand = cand_u ^ int_min
        n_ge = count(lambda keys, _: keys >= cand)
        ok = n_ge >= n_keep
        return jnp.where(ok, cand_u, thr_u), jnp.where(ok, n_ge, n_at)

    prefix = lax.shift_left(d1, jnp.int32(32 - DIGIT_BITS)) | lax.shift_left(d2, jnp.int32(32 - 2 * DIGIT_BITS))
    n_prefix = n_gt1 + n_at2
    thr_ref[...] = prefix

    @pl.when(jnp.max((n_prefix - n_keep).astype(F32)) > 0.0)
    def _():
        thr_u, n_at = lax.fori_loop(2 * DIGIT_BITS, 32, thr_bit, (prefix, n_prefix))
        thr_ref[...] = thr_u
        thr = thr_u ^ int_min

        @pl.when(jnp.max((n_at - n_keep).astype(F32)) > 0.0)
        def _():
            need = n_keep - count(lambda keys, _: keys > thr)
            n_pos_bits = int(math.ceil(math.log2(key_ref.shape[0] + 1)))

            def pos_bit(i, lim):
                cand = lim + lax.shift_left(jnp.int32(1), n_pos_bits - 1 - i)
                n_tied = count(lambda keys, start: (keys == thr) & (row + start < cand))
                return jnp.where(n_tied <= need, cand, lim)

            lim = lax.fori_loop(0, n_pos_bits, pos_bit, jnp.zeros((1, T), I32))

            def drop(j, _):
                start = pl.multiple_of(j * C, C)
                keys = key_ref[pl.ds(start, C), :]
                key_ref[pl.ds(start, C), :] = jnp.where((keys == thr) & (row + start >= lim), int_min, keys)
                return 0

            lax.fori_loop(0, n_chunks, drop, 0)

    thr = thr_ref[...] ^ int_min

    def key_mask(start, far):
        keep = key_ref[pl.ds(start, C), :] >= thr
        if far:
            keep = keep & (row + start < t0 - T)
        return jnp.where(keep, 0.0, NEG)

    load_k = lambda start: k_ref[0, pl.ds(start, C), :]
    load_vt = lambda start: vt_ref[0, :, pl.ds(start, C)]
    diag_start = pl.multiple_of(jnp.maximum(t0 - T, 0), T)
    o = _attend_finish(_far_attend(qi, H, q_ref, load_k, load_vt, lambda start: key_mask(start, True),
                                   diag_start, key_mask(diag_start, False), tdiag_ref[0], s_ref, p_ref))
    for h in range(H):
        o_ref[0, :, h * HEAD_DIM:(h + 1) * HEAD_DIM] = o[:, h * T:(h + 1) * T].T.astype(o_ref.dtype)


def _dsa(q, iq, iw_t, ik, k, vt, tdiag):
    b, _, s, _ = q.shape
    assert s % (2 * KEY_CHUNK) == 0
    nq = s // Q_BLOCK
    n_keep = min(DSA_TOPK_MAX, s // 4)
    heads = lambda: pl.BlockSpec((1, DSA_HEADS, Q_BLOCK, HEAD_DIM), lambda bi, qi: (bi, 0, qi, 0))
    rows = lambda: pl.BlockSpec((1, s, HEAD_DIM), lambda bi, qi: (bi, 0, 0))
    return pl.pallas_call(
        functools.partial(_dsa_kernel, n_keep=n_keep),
        grid=(b, nq),
        in_specs=[heads(), heads(),
                  pl.BlockSpec((1, IDX_HEADS, Q_BLOCK), lambda bi, qi: (bi, 0, qi)),
                  rows(), rows(),
                  pl.BlockSpec((1, V_ROWS, s), lambda bi, qi: (bi, 0, 0)),
                  pl.BlockSpec((1, KEY_CHUNK, DSA_HEADS * Q_BLOCK), lambda bi, qi: (jnp.minimum(qi, 1), 0, 1))],
        out_specs=pl.BlockSpec((1, Q_BLOCK, DSA_HEADS * HEAD_DIM), lambda bi, qi: (bi, qi, 0)),
        out_shape=jax.ShapeDtypeStruct((b, s, DSA_HEADS * HEAD_DIM), BF16),
        scratch_shapes=[pltpu.VMEM((s, Q_BLOCK), I32), pltpu.VMEM((s // 2, Q_BLOCK), I32),
                        pltpu.VMEM((1, Q_BLOCK), I32),
                        pltpu.VMEM((2, KEY_CHUNK, DSA_HEADS * Q_BLOCK), F32),
                        pltpu.VMEM((2, KEY_CHUNK, DSA_HEADS * Q_BLOCK), BF16)],
        compiler_params=_params("parallel", "arbitrary"),
        name="dsa",
    )(q, iq, iw_t, ik, k, vt, tdiag)


def _merge_kernel(x_ref, oa_ref, ob_ref, ga_ref, gb_ref, wa_ref, wb_ref, wo_ref, gn_ref, wr_ref, br_ref,
                  x1_ref, xn_ref, gate_ref, gate_t_ref):
    merged = ga_ref[...] * _dot(oa_ref[0], wa_ref[...]) + gb_ref[...] * _dot(ob_ref[0], wb_ref[...])
    x1 = x_ref[...] + _dot(merged.astype(BF16), wo_ref[...])
    x1_ref[...] = x1
    xn = _row_rms(x1, gn_ref[...]).astype(BF16)
    xn_ref[...] = xn

    logits = _dot(xn, wr_ref[...]) + br_ref[...]
    lane = lax.broadcasted_iota(I32, logits.shape, 1)
    work = jnp.where(lane < N_EXPERTS, logits, -3e38)
    gates = jnp.zeros_like(logits)
    denom = jnp.zeros((logits.shape[0], 1), F32)
    top = None
    for _ in range(MOE_TOPK):
        mx = jnp.max(work, axis=1, keepdims=True)
        first = jnp.min(jnp.where(work == mx, lane, LANES), axis=1, keepdims=True)
        hit = lane == first
        top = mx if top is None else top
        e = jnp.exp(mx - top)
        gates = jnp.where(hit, e, gates)
        denom = denom + e
        work = jnp.where(hit, -3e38, work)
    gates = gates / denom
    gate_ref[...] = gates
    gate_t_ref[...] = gates.T


def _merge(x, oa, ob, ga, gb, w_branch_a, w_branch_b, w_out, ffn_norm, w_router, b_router, tm):
    b, s, d = x.shape
    n = b * s
    xf = x.reshape(n, d)
    nt = s // tm
    wa = w_branch_a.astype(BF16)
    wb = w_branch_b.astype(BF16)
    wr = jnp.concatenate([w_router, jnp.zeros((d, LANES - N_EXPERTS), F32)], axis=1).astype(BF16)
    br = jnp.concatenate([b_router, jnp.zeros((LANES - N_EXPERTS,), F32)]).reshape(1, LANES)
    consts = [wa, wb, w_out.astype(BF16), ffn_norm.reshape(1, d), wr, br]
    row = lambda w: pl.BlockSpec((tm, w), lambda bi, ti: (bi * nt + ti, 0))
    heads = lambda: pl.BlockSpec((1, tm, NSA_HEADS * HEAD_DIM), lambda bi, ti: (bi, ti, 0))
    return pl.pallas_call(
        _merge_kernel,
        grid=(b, nt),
        in_specs=[row(d), heads(), heads(), row(d), row(d)] + [_const_spec(c.shape) for c in consts],
        out_specs=[row(d), row(d), row(LANES), pl.BlockSpec((LANES, tm), lambda bi, ti: (0, bi * nt + ti))],
        out_shape=[jax.ShapeDtypeStruct((n, d), F32), jax.ShapeDtypeStruct((n, d), BF16),
                   jax.ShapeDtypeStruct((n, LANES), F32), jax.ShapeDtypeStruct((LANES, n), F32)],
        compiler_params=_params("parallel", "parallel"),
        name="merge",
    )(xf, oa, ob, ga, gb, *consts)


def _deinterleave_index(width):
    half = MXU_DIM // 2
    c = np.arange(width)
    r = c % MXU_DIM
    return (c - r) + np.where(r < half, 2 * r, 2 * (r - half) + 1)


def _moe_prep_kernel(w_ref, perm_ref, o_ref):
    perm = perm_ref[...]
    for g in range(w_ref.shape[2] // MXU_DIM):
        cols = slice(g * MXU_DIM, (g + 1) * MXU_DIM)
        o_ref[0, :, cols] = _dot(w_ref[0, :, cols].astype(BF16), perm).astype(BF16)


def _moe_prep(w_gate_up, tr):
    e, d, f2 = w_gate_up.shape
    src = _deinterleave_index(MXU_DIM)
    perm = jnp.asarray((np.arange(MXU_DIM)[:, None] == src[None, :]).astype(np.float32), BF16)
    return pl.pallas_call(
        _moe_prep_kernel,
        grid=(e, d // tr),
        in_specs=[pl.BlockSpec((1, tr, f2), lambda i, j: (i, j, 0)), _const_spec(perm.shape)],
        out_specs=pl.BlockSpec((1, tr, f2), lambda i, j: (i, j, 0)),
        out_shape=jax.ShapeDtypeStruct((e, d, f2), BF16),
        compiler_params=_params("parallel", "parallel"),
        name="moe_prep",
    )(w_gate_up, perm)


def _moe_kernel(x1_ref, xn_ref, gate_ref, gate_t_ref, lower_ref, upper_ref, wgu_ref, bgu_ref, wd_ref, bd_ref,
                o_ref, rank_hi_ref, rank_lo_ref, rank_t_ref):
    e = pl.program_id(1)
    tm, R = xn_ref.shape[0], MOE_ROWS

    @pl.when(e == 0)
    def _():
        o_ref[...] = x1_ref[...]
        routed = jnp.where(gate_ref[...] > 0.0, 1.0, 0.0).astype(BF16)
        routed_t = jnp.where(gate_t_ref[...] > 0.0, 1.0, 0.0).astype(BF16)
        rank = jnp.where(gate_ref[...] > 0.0, _dot(lower_ref[...], routed), -1.0)
        hi = rank.astype(BF16)
        rank_hi_ref[...] = hi
        rank_lo_ref[...] = (rank - hi.astype(F32)).astype(BF16)
        rank_t_ref[...] = _dot(routed_t, upper_ref[...])

    g_row = gate_t_ref[pl.ds(e, 1), :]
    rank_row = jnp.where(g_row > 0.0, rank_t_ref[pl.ds(e, 1), :], -1.0)
    take_e = jnp.where(lax.broadcasted_iota(I32, (LANES, LANES), 0) == e, 1.0, 0.0).astype(BF16)
    rank_rep = _dot(rank_hi_ref[...], take_e) + _dot(rank_lo_ref[...], take_e)
    rank_cols = jnp.concatenate([rank_rep] * (R // LANES) + [rank_rep[:, :R % LANES]], axis=1)
    n_routed = jnp.sum(jnp.where(g_row > 0.0, 1.0, 0.0)).astype(I32)
    slot_r = lax.broadcasted_iota(I32, (R, tm), 0).astype(F32)
    slot_c = lax.broadcasted_iota(I32, (tm, R), 1).astype(F32)
    half = MXU_DIM // 2

    def block(j, _):
        base = (j * R).astype(F32)
        pick = jnp.where(rank_row - base == slot_r, 1.0, 0.0)
        xg = _dot(pick.astype(BF16), xn_ref[...]).astype(BF16)
        hu = _dot(xg, wgu_ref[0]) + bgu_ref[0]
        acts = []
        for g in range(hu.shape[1] // MXU_DIM):
            glu = jnp.minimum(hu[:, g * MXU_DIM:g * MXU_DIM + half], SWIGLU_LIMIT)
            lin = jnp.clip(hu[:, g * MXU_DIM + half:(g + 1) * MXU_DIM], -SWIGLU_LIMIT, SWIGLU_LIMIT)
            acts.append((glu * jax.nn.sigmoid(SWIGLU_ALPHA * glu) * (lin + 1.0)).astype(BF16))
        y = _dot(jnp.concatenate(acts, axis=1), wd_ref[0]) + bd_ref[0]
        g_r = jnp.sum(pick * g_row, axis=1, keepdims=True)
        put = jnp.where(rank_cols - base == slot_c, 1.0, 0.0).astype(BF16)
        o_ref[...] += _dot(put, (y * g_r).astype(BF16))
        return 0

    lax.fori_loop(0, (n_routed + R - 1) // R, block, 0)


def _moe(x1, xn, gates, gates_t, w_gate_up, b_gate_up, w_down, b_down, tm):
    n, d = x1.shape
    f = D_EXPERT
    wgu = _moe_prep(w_gate_up, tr=256)
    bgu = b_gate_up[:, _deinterleave_index(2 * f)].reshape(N_EXPERTS, 1, 2 * f)
    wd = w_down.astype(BF16)
    bd = b_down.reshape(N_EXPERTS, 1, d)
    earlier = np.arange(tm)[None, :] < np.arange(tm)[:, None]
    lower = jnp.asarray(earlier.astype(np.float32), BF16)
    upper = jnp.asarray(earlier.T.astype(np.float32), BF16)
    row = lambda w: pl.BlockSpec((tm, w), lambda i, e: (i, 0))
    per_e = lambda a, b_: pl.BlockSpec((1, a, b_), lambda i, e: (e, 0, 0))
    return pl.pallas_call(
        _moe_kernel,
        grid=(n // tm, N_EXPERTS),
        in_specs=[row(d), row(d), row(LANES), pl.BlockSpec((LANES, tm), lambda i, e: (0, i)),
                  _const_spec(lower.shape), _const_spec(upper.shape),
                  per_e(d, 2 * f), per_e(1, 2 * f), per_e(f, d), per_e(1, d)],
        out_specs=row(d),
        out_shape=jax.ShapeDtypeStruct((n, d), F32),
        scratch_shapes=[pltpu.VMEM((tm, LANES), BF16), pltpu.VMEM((tm, LANES), BF16), pltpu.VMEM((LANES, tm), F32)],
        compiler_params=_params("parallel", "arbitrary"),
        name="moe",
    )(x1, xn, gates, gates_t, lower, upper, wgu, bgu, wd, bd)


def _rel_bucket(rel):
    n = np.maximum(rel, 0)
    exact = REL_BUCKETS // 2
    nf = np.maximum(n, 1).astype(np.float64)
    large = exact + (np.log(nf / exact) / math.log(REL_MAX_DIST / exact) * (REL_BUCKETS - exact)).astype(np.int64)
    large = np.minimum(large, REL_BUCKETS - 1)
    return np.where(n < exact, n, large)


def _rel_table(rel_bias, rel):
    far = rel_bias[REL_BUCKETS - 1]
    one_hot = jax.nn.one_hot(jnp.asarray(_rel_bucket(rel).reshape(-1)), REL_BUCKETS, dtype=F32)
    picked = jnp.dot(one_hot, rel_bias, precision=lax.Precision.HIGHEST).reshape(rel.shape + (-1,))
    tab = jnp.where(jnp.asarray(rel)[..., None] >= 0, (picked - far) * LOG2E, NEG)
    tab = jnp.moveaxis(tab, -1, -2)
    return tab.reshape(tab.shape[:-2] + (-1,)).astype(F32)


def _bias_tables(rel_bias):
    i = np.arange(Q_BLOCK)[None, :]
    j = np.arange(KEY_CHUNK)[:, None]
    tdiag = _rel_table(rel_bias, np.stack([i - j, i - j + Q_BLOCK]))
    r = np.arange(CMP_BAND)[:, None]
    rel_band = np.stack([i - (CMP_LEN - 1) - CMP_STRIDE * (c0 + r) for c0 in (0, -SUBLANES, -2 * SUBLANES)])
    band = _rel_table(rel_bias[:, :NSA_HEADS], rel_band)
    return tdiag, band


def _overlap_matrix_t(s):
    n_cp = s // CMP_STRIDE
    n_s = s // SLC_LEN
    c_start = np.arange(n_cp) * CMP_STRIDE
    s_start = np.arange(n_s) * SLC_LEN
    ovl = (c_start[None, :] < s_start[:, None] + SLC_LEN) & (c_start[None, :] + CMP_LEN > s_start[:, None])
    ovl[:, n_cp - 1] = False
    return jnp.asarray(ovl.astype(np.float32), BF16)


def _layer(x, attn_norm, w_in, nsa_q_norm, nsa_k_norm, cmp_k_pe, cmp_k_w1, cmp_k_b1, cmp_k_w2,
           cmp_v_pe, cmp_v_w1, cmp_v_b1, cmp_v_w2, dsa_kv_norm, dsa_w_uk, dsa_w_uv, dsa_q_norm,
           dsa_k_norm, w_branch_a, w_branch_b, w_out, rel_bias, ffn_norm, w_router, b_router,
           w_gate_up, b_gate_up, w_down, b_down):
    b, s, d = x.shape
    n = b * s
    g = NSA_KV_HEADS
    (qa, ck, cv, ks, sv_t, kw, wv_t, misc_t, qb, kb, vb_t, iq, ik, ga, gb) = _in_proj(
        x, attn_norm, w_in, nsa_q_norm, nsa_k_norm, dsa_q_norm, dsa_kv_norm, dsa_w_uk, dsa_w_uv, dsa_k_norm,
        tm=min(512, s))

    chunks = lambda a: a.reshape(b, g, s // CMP_STRIDE, CMP_STRIDE * HEAD_DIM)
    kc = _compress(chunks(ck), cmp_k_pe, cmp_k_w1, cmp_k_b1, cmp_k_w2, nsa_k_norm, True)
    vc = _compress(chunks(cv), cmp_v_pe, cmp_v_w1, cmp_v_b1, cmp_v_w2, nsa_k_norm, False)

    tdiag, band = _bias_tables(rel_bias)
    gates_t = misc_t[:, :NSA_HEADS * 3].reshape(b, g, NSA_GROUP * 3, s)
    overlap_t = _overlap_matrix_t(s)
    vc_t = vc.transpose(0, 1, 3, 2)
    vcx = jnp.concatenate([vc_t, jnp.ones((b, g, V_ROWS - HEAD_DIM, vc_t.shape[3]), BF16),
                           jnp.broadcast_to(overlap_t, (b, g) + overlap_t.shape)], axis=2)
    oa = _nsa(qa, kc, vcx, ks, sv_t, kw, wv_t, gates_t, band, tdiag)

    iw_t = misc_t[:, NSA_HEADS * 3:NSA_HEADS * 3 + IDX_HEADS]
    ob = _dsa(qb, iq, iw_t, ik.reshape(b, s, IDX_DIM), kb.reshape(b, s, HEAD_DIM), vb_t, tdiag)

    x1, xn, moe_gates, moe_gates_t = _merge(x, oa, ob, ga, gb, w_branch_a, w_branch_b, w_out, ffn_norm, w_router,
                               b_router, tm=min(512, s))
    out = _moe(x1, xn, moe_gates, moe_gates_t, w_gate_up, b_gate_up, w_down, b_down, tm=min(MOE_TOKENS, n))
    return out.reshape(b, s, d)


def kernel(x, attn_norm, w_in, nsa_q_norm, nsa_k_norm, cmp_k_pe, cmp_k_w1, cmp_k_b1, cmp_k_w2, cmp_v_pe, cmp_v_w1, cmp_v_b1, cmp_v_w2, dsa_kv_norm, dsa_w_uk, dsa_w_uv, dsa_q_norm, dsa_k_norm, w_branch_a, w_branch_b, w_out, rel_bias, ffn_norm, w_router, b_router, w_gate_up, b_gate_up, w_down, b_down):
    for l in range(attn_norm.shape[0]):
        x = _layer(x, attn_norm[l], w_in[l], nsa_q_norm[l], nsa_k_norm[l], cmp_k_pe[l], cmp_k_w1[l],
                   cmp_k_b1[l], cmp_k_w2[l], cmp_v_pe[l], cmp_v_w1[l], cmp_v_b1[l], cmp_v_w2[l],
                   dsa_kv_norm[l], dsa_w_uk[l], dsa_w_uv[l], dsa_q_norm[l], dsa_k_norm[l],
                   w_branch_a[l], w_branch_b[l], w_out[l], rel_bias, ffn_norm[l], w_router[l],
                   b_router[l], w_gate_up[l], b_gate_up[l], w_down[l], b_down[l])
    return x
```
